```python
import math
import jax, jax.numpy as jnp
from jax import lax
import numpy as np

D_MODEL = 1024
BATCH = 32
SEQ = 2048
DEPTH = 2

D_MIX = D_MODEL
CONV_CH = D_MIX // 4
CONV_WIDTH = 31
MLA_HEADS = 6
MLA_NOPE = 64
MLA_ROPE = 32
MLA_V = 64
MLA_Q_RANK = 256
MLA_KV_RANK = 128
DIFF_HEADS = 6
DIFF_HD = 32
DIFF_V = 2 * DIFF_HD
IN_COLS = 2 * CONV_CH + MLA_Q_RANK + MLA_KV_RANK + MLA_ROPE + 3 * DIFF_HEADS * DIFF_V
D_FF = 2816
N_EXPERTS = 8
TOP_K = 2
N_DENSE = (DEPTH + 1) // 2
N_MOE = DEPTH // 2
Q_BLOCK = 128
ROPE_THETA = 10000.0
RMS_EPS = 1e-6
LN_EPS = 1e-5

kernel_name = 'hybrid_conv_mla_diffattn_moe'


def rms_norm(x, g, eps=RMS_EPS):
    xf = x.astype(jnp.float32)
    y = xf * lax.rsqrt(jnp.mean(xf * xf, axis=-1, keepdims=True) + eps)
    return (y * g.astype(jnp.float32)).astype(x.dtype)


def layer_norm(x, g, b, eps=LN_EPS):
    xf = x.astype(jnp.float32)
    mu = jnp.mean(xf, axis=-1, keepdims=True)
    var = jnp.mean(jnp.square(xf - mu), axis=-1, keepdims=True)
    y = (xf - mu) * lax.rsqrt(var + eps)
    return (y * g.astype(jnp.float32) + b.astype(jnp.float32)).astype(x.dtype)


def apply_rope(x, pos):
    half = x.shape[-1] // 2
    freq = ROPE_THETA ** (-jnp.arange(half, dtype=jnp.float32) / half)
    ang = pos.astype(jnp.float32)[:, None] * freq[None, :]
    cos = jnp.cos(ang).astype(x.dtype)
    sin = jnp.sin(ang).astype(x.dtype)
    x1, x2 = x[..., :half], x[..., half:]
    return jnp.concatenate([x1 * cos - x2 * sin, x1 * sin + x2 * cos], axis=-1)


def alibi_slopes(n_heads):
    return 2.0 ** (-8.0 * jnp.arange(1, n_heads + 1, dtype=jnp.float32) / n_heads)


def causal_block_scores(q_blk, k_ctx, q_start, scale, slopes=None):
    s = jnp.einsum('bhqd,bhkd->bhqk', q_blk, k_ctx).astype(jnp.float32) * scale
    q_pos = q_start + jnp.arange(q_blk.shape[2])
    k_pos = jnp.arange(k_ctx.shape[2])
    dist = q_pos[:, None] - k_pos[None, :]
    if slopes is not None:
        s = s - slopes[:, None, None] * dist.astype(jnp.float32)[None]
    return jnp.where(dist >= 0, s, -jnp.inf)


def mla_attention(q, k, v):
    S = q.shape[2]
    scale = (MLA_NOPE + MLA_ROPE) ** -0.5
    outs = []
    for i in range(S // Q_BLOCK):
        start, end = i * Q_BLOCK, (i + 1) * Q_BLOCK
        p = jax.nn.softmax(causal_block_scores(q[:, :, start:end], k[:, :, :end], start, scale), axis=-1)
        outs.append(jnp.einsum('bhqk,bhkd->bhqd', p.astype(v.dtype), v[:, :, :end]))
    return jnp.concatenate(outs, axis=2)


def diff_attention(q1, q2, k1, k2, v, lam):
    S = q1.shape[2]
    scale = DIFF_HD ** -0.5
    slopes = alibi_slopes(DIFF_HEADS)
    outs = []
    for i in range(S // Q_BLOCK):
        start, end = i * Q_BLOCK, (i + 1) * Q_BLOCK
        p1 = jax.nn.softmax(causal_block_scores(q1[:, :, start:end], k1[:, :, :end], start, scale, slopes), axis=-1)
        p2 = jax.nn.softmax(causal_block_scores(q2[:, :, start:end], k2[:, :, :end], start, scale, slopes), axis=-1)
        a = p1 - lam * p2
        outs.append(jnp.einsum('bhqk,bhkd->bhqd', a.astype(v.dtype), v[:, :, :end]))
    return jnp.concatenate(outs, axis=2)


def conv_module(a, b_glu, w_dw, b_dw, ln_g, ln_b, w_pw, b_pw):
    a = a + b_glu
    u = a[..., :CONV_CH] * jax.nn.sigmoid(a[..., CONV_CH:])
    c = lax.conv_general_dilated(
        u, w_dw[:, None, :].astype(u.dtype), window_strides=(1,),
        padding=[(CONV_WIDTH - 1, 0)], dimension_numbers=('NWC', 'WIO', 'NWC'),
        feature_group_count=CONV_CH) + b_dw
    c = jax.nn.silu(layer_norm(c, ln_g, ln_b))
    return c @ w_pw + b_pw


def swiglu(h, w_gate, w_up, w_down):
    return (jax.nn.silu(h @ w_gate) * (h @ w_up)) @ w_down


def moe_ffn(h, w_router, w_gate, w_up, w_down):
    B, S, D = h.shape
    hf = h.reshape(B * S, D)
    logits = (hf @ w_router).astype(jnp.float32)
    top_vals, top_idx = lax.top_k(logits, TOP_K)
    top_w = jax.nn.softmax(top_vals, axis=-1)
    gates = jnp.einsum('nk,nke->ne', top_w, jax.nn.one_hot(top_idx, N_EXPERTS, dtype=jnp.float32))
    out = jnp.zeros_like(hf)
    for e in range(N_EXPERTS):
        out = out + gates[:, e:e + 1].astype(hf.dtype) * swiglu(hf, w_gate[e], w_up[e], w_down[e])
    return out.reshape(B, S, D)


def hybrid_mixer(h, layer_idx, w_in, b_glu, w_dw, b_dw, conv_ln_g, conv_ln_b, w_pw, b_pw,
                 q_norm, w_uq, kv_norm, w_ukv, lambda_q1, lambda_k1, lambda_q2, lambda_k2,
                 diff_subln, w_out):
    B, S, _ = h.shape
    pos = jnp.arange(S)
    proj = h @ w_in
    cuts = [int(c) for c in np.cumsum([2 * CONV_CH, MLA_Q_RANK, MLA_KV_RANK, MLA_ROPE,
                                        DIFF_HEADS * DIFF_V, DIFF_HEADS * DIFF_V])]
    conv_in, c_q, c_kv, k_r, d_q, d_k, d_v = jnp.split(proj, cuts, axis=-1)

    y_conv = conv_module(conv_in, b_glu, w_dw, b_dw, conv_ln_g, conv_ln_b, w_pw, b_pw)

    q = (rms_norm(c_q, q_norm) @ w_uq).reshape(B, S, MLA_HEADS, MLA_NOPE + MLA_ROPE).transpose(0, 2, 1, 3)
    q = jnp.concatenate([q[..., :MLA_NOPE], apply_rope(q[..., MLA_NOPE:], pos)], axis=-1)
    kv = (rms_norm(c_kv, kv_norm) @ w_ukv).reshape(B, S, MLA_HEADS, MLA_NOPE + MLA_V).transpose(0, 2, 1, 3)
    k_rope = apply_rope(k_r[:, None], pos)
    k = jnp.concatenate([kv[..., :MLA_NOPE], jnp.broadcast_to(k_rope, (B, MLA_HEADS, S, MLA_ROPE))], axis=-1)
    v = kv[..., MLA_NOPE:]
    y_mla = mla_attention(q, k, v).transpose(0, 2, 1, 3).reshape(B, S, MLA_HEADS * MLA_V)

    def heads(t):
        return t.reshape(B, S, DIFF_HEADS, DIFF_V).transpose(0, 2, 1, 3)
    dq, dk, dv = heads(d_q), heads(d_k), heads(d_v)
    lam_init = 0.8 - 0.6 * math.exp(-0.3 * layer_idx)
    lam = (jnp.exp(jnp.sum(lambda_q1.astype(jnp.float32) * lambda_k1.astype(jnp.float32)))
           - jnp.exp(jnp.sum(lambda_q2.astype(jnp.float32) * lambda_k2.astype(jnp.float32)))
           + lam_init)
    o = diff_attention(dq[..., :DIFF_HD], dq[..., DIFF_HD:], dk[..., :DIFF_HD], dk[..., DIFF_HD:], dv, lam)
    o = rms_norm(o, diff_subln, LN_EPS) * (1.0 - lam_init)
    y_diff = o.transpose(0, 2, 1, 3).reshape(B, S, DIFF_HEADS * DIFF_V)

    return jnp.concatenate([y_conv, y_mla, y_diff], axis=-1) @ w_out


def setup_inputs(seed: int = 0) -> dict:
    key = jax.random.key(seed)
    ks = list(jax.random.split(key, 32))

    def nrm(shape, scale):
        return scale * jax.random.normal(ks.pop(), shape, jnp.float32)

    def gain(shape):
        return 1.0 + nrm(shape, 0.02)

    D, C, L = D_MODEL, CONV_CH, DEPTH
    return {
        'x': nrm((BATCH, SEQ, D), 1.0),
        'attn_norm': gain((L, D)),
        'w_in': nrm((L, D, IN_COLS), D ** -0.5),
        'b_glu': nrm((L, 2 * C), 0.02),
        'w_dw': nrm((L, CONV_WIDTH, C), CONV_WIDTH ** -0.5),
        'b_dw': nrm((L, C), 0.02),
        'conv_ln_g': gain((L, C)),
        'conv_ln_b': nrm((L, C), 0.02),
        'w_pw': nrm((L, C, C), C ** -0.5),
        'b_pw': nrm((L, C), 0.02),
        'q_norm': gain((L, MLA_Q_RANK)),
        'w_uq': nrm((L, MLA_Q_RANK, MLA_HEADS * (MLA_NOPE + MLA_ROPE)), MLA_Q_RANK ** -0.5),
        'kv_norm': gain((L, MLA_KV_RANK)),
        'w_ukv': nrm((L, MLA_KV_RANK, MLA_HEADS * (MLA_NOPE + MLA_V)), MLA_KV_RANK ** -0.5),
        'lambda_q1': nrm((L, DIFF_HD), 0.1),
        'lambda_k1': nrm((L, DIFF_HD), 0.1),
        'lambda_q2': nrm((L, DIFF_HD), 0.1),
        'lambda_k2': nrm((L, DIFF_HD), 0.1),
        'diff_subln': gain((L, DIFF_V)),
        'w_out': nrm((L, D_MIX, D), D_MIX ** -0.5),
        'ffn_norm': gain((L, D)),
        'w_gate_dense': nrm((N_DENSE, D, D_FF), D ** -0.5),
        'w_up_dense': nrm((N_DENSE, D, D_FF), D ** -0.5),
        'w_down_dense': nrm((N_DENSE, D_FF, D), D_FF ** -0.5),
        'w_router': nrm((N_MOE, D, N_EXPERTS), D ** -0.5),
        'w_gate_exp': nrm((N_MOE, N_EXPERTS, D, D_FF), D ** -0.5),
        'w_up_exp': nrm((N_MOE, N_EXPERTS, D, D_FF), D ** -0.5),
        'w_down_exp': nrm((N_MOE, N_EXPERTS, D_FF, D), D_FF ** -0.5),
        'final_norm': gain((D,)),
    }


def reference(x, attn_norm, w_in, b_glu, w_dw, b_dw, conv_ln_g, conv_ln_b, w_pw, b_pw,
              q_norm, w_uq, kv_norm, w_ukv, lambda_q1, lambda_k1, lambda_q2, lambda_k2,
              diff_subln, w_out, ffn_norm, w_gate_dense, w_up_dense, w_down_dense,
              w_router, w_gate_exp, w_up_exp, w_down_exp, final_norm):
    for l in range(DEPTH):
        h = rms_norm(x, attn_norm[l])
        x = x + hybrid_mixer(h, l, w_in[l], b_glu[l], w_dw[l], b_dw[l], conv_ln_g[l], conv_ln_b[l],
                             w_pw[l], b_pw[l], q_norm[l], w_uq[l], kv_norm[l], w_ukv[l],
                             lambda_q1[l], lambda_k1[l], lambda_q2[l], lambda_k2[l],
                             diff_subln[l], w_out[l])
        h = rms_norm(x, ffn_norm[l])
        j = l // 2
        if l % 2 == 0:
            x = x + swiglu(h, w_gate_dense[j], w_up_dense[j], w_down_dense[j])
        else:
            x = x + moe_ffn(h, w_router[j], w_gate_exp[j], w_up_exp[j], w_down_exp[j])
    return rms_norm(x, final_norm)
```

```python
import functools
import math

import numpy as np
import jax
import jax.numpy as jnp
from jax import lax
from jax.experimental import pallas as pl
from jax.experimental.pallas import tpu as pltpu

MLA_HEADS = 6
MLA_NOPE = 64
MLA_ROPE = 32
MLA_V = 64
DIFF_HEADS = 6
DIFF_HD = 32
DIFF_V = 2 * DIFF_HD
TOP_K = 2
ROPE_THETA = 10000.0
RMS_EPS = 1e-6
LN_EPS = 1e-5

LANE = 128
NEG_BIG = -1e30
VMEM_LIMIT = 56 * 1024 * 1024

F32 = jnp.float32
BF16 = jnp.bfloat16

HEAD_PAIRS = MLA_HEADS // 2
assert MLA_HEADS == DIFF_HEADS and MLA_HEADS % 2 == 0
assert MLA_V == DIFF_V == LANE // 2 and MLA_NOPE + MLA_ROPE <= LANE


def _rms(x, g, eps):
    return x * lax.rsqrt(jnp.mean(x * x, axis=-1, keepdims=True) + eps) * g


def _dot(a, b):
    return jnp.dot(a, b, preferred_element_type=F32)


def _dot_nt(a, b):
    return lax.dot_general(a, b, (((1,), (1,)), ((), ())), preferred_element_type=F32)


def _params(*sem):
    return pltpu.CompilerParams(dimension_semantics=sem, vmem_limit_bytes=VMEM_LIMIT)


C_CONV = 0
C_CQ = 512
C_CKV = 768
C_KRA = 896
C_KRB = 1024
C_DQ = 1152
C_DK = 1536
C_DV = 1920
C_END = 2304


def _pre_mixer_kernel(x_ref, g_ref, win_ref, bglu_ref, qn_ref, wuq_ref, kvn_ref, wukv_ref, vones_ref,
                      pq_ref, pk_ref, cos_ref, sin_ref,
                      u_ref, q_ref, k_ref, v_ref, dq_ref, dk_ref, dv_ref, *, mla_scale, diff_scale, conv_ch):
    h = _rms(x_ref[...], g_ref[...], RMS_EPS).astype(BF16)
    proj = _dot(h, win_ref[...])

    a = proj[:, C_CONV:C_CONV + 2 * conv_ch] + bglu_ref[...]
    u_ref[...] = a[:, :conv_ch] * jax.nn.sigmoid(a[:, conv_ch:])

    dq_ref[...] = (proj[:, C_DQ:C_DK] * diff_scale).astype(BF16)
    dk_ref[...] = proj[:, C_DK:C_DV].astype(BF16)
    dv_ref[...] = proj[:, C_DV:C_END].astype(BF16)

    cos = cos_ref[...]
    sin = sin_ref[...]

    hq = _rms(proj[:, C_CQ:C_CKV], qn_ref[...], RMS_EPS).astype(BF16)
    qall = _dot(hq, wuq_ref[...])
    npad = MLA_HEADS * LANE
    r1 = qall[:, npad:npad + LANE]
    r2 = qall[:, npad + LANE:npad + 2 * LANE]
    qrot = (jnp.concatenate([r1 * cos - r2 * sin, r1 * sin + r2 * cos], axis=-1) * mla_scale).astype(BF16)
    q_ref[...] = (qall[:, :npad] * mla_scale + _dot(qrot, pq_ref[...])).astype(BF16)

    hkv = _rms(proj[:, C_CKV:C_KRA], kvn_ref[...], RMS_EPS).astype(BF16)
    kv = _dot(hkv, wukv_ref[...])
    k1 = proj[:, C_KRA:C_KRB]
    k2 = proj[:, C_KRB:C_DQ]
    krot = jnp.concatenate([k1 * cos - k2 * sin, k1 * sin + k2 * cos], axis=-1).astype(BF16)
    k_ref[...] = (kv[:, :npad] + _dot(krot, pk_ref[...])).astype(BF16)
    v_ref[...] = (kv[:, npad:] + vones_ref[...]).astype(BF16)


def _prep_in_weights(w_in, w_uq, w_ukv):
    d = w_in.shape[0]
    cuts = np.cumsum([512, 256, 128, 32, 384, 384])
    conv, cq, ckv, kr, dq, dk, dv = jnp.split(w_in, [int(c) for c in cuts], axis=-1)
    half = MLA_ROPE // 2
    zpad = jnp.zeros((d, LANE - half), w_in.dtype)
    win_p = jnp.concatenate([conv, cq, ckv, kr[:, :half], zpad, kr[:, half:], zpad, dq, dk, dv], axis=-1)
    assert win_p.shape[1] == C_END

    qr = MLA_NOPE + MLA_ROPE
    rq = w_uq.shape[0]
    wq = w_uq.reshape(rq, MLA_HEADS, qr)
    nope = jnp.pad(wq[:, :, :MLA_NOPE], ((0, 0), (0, 0), (0, LANE - MLA_NOPE))).reshape(rq, MLA_HEADS * LANE)
    r1 = jnp.pad(wq[:, :, MLA_NOPE:MLA_NOPE + half].reshape(rq, MLA_HEADS * half), ((0, 0), (0, LANE - MLA_HEADS * half)))
    r2 = jnp.pad(wq[:, :, MLA_NOPE + half:].reshape(rq, MLA_HEADS * half), ((0, 0), (0, LANE - MLA_HEADS * half)))
    wuq_p = jnp.concatenate([nope, r1, r2], axis=-1)

    rkv = w_ukv.shape[0]
    wkv = w_ukv.reshape(rkv, MLA_HEADS, MLA_NOPE + MLA_V)
    kn = jnp.pad(wkv[:, :, :MLA_NOPE], ((0, 0), (0, 0), (0, LANE - MLA_NOPE)))
    vv = wkv[:, :, MLA_NOPE:]
    v_even = jnp.pad(vv, ((0, 0), (0, 0), (0, LANE - MLA_V)))
    v_odd = jnp.pad(vv, ((0, 0), (0, 0), (LANE - MLA_V, 0)))
    odd = (jnp.arange(MLA_HEADS) % 2 == 1)[None, :, None]
    vp = jnp.where(odd, v_odd, v_even)
    wukv_p = jnp.concatenate([kn.reshape(rkv, -1), vp.reshape(rkv, -1)], axis=-1)
    return win_p.astype(BF16), wuq_p.astype(BF16), wukv_p.astype(BF16)


def _static_tables(seq):
    half = MLA_ROPE // 2
    freq = ROPE_THETA ** (-jnp.arange(half, dtype=F32) / half)
    ang = jnp.arange(seq, dtype=F32)[:, None] * freq[None, :]
    reps = MLA_HEADS
    cos = jnp.pad(jnp.tile(jnp.cos(ang), (1, reps)), ((0, 0), (0, LANE - reps * half)))
    sin = jnp.pad(jnp.tile(jnp.sin(ang), (1, reps)), ((0, 0), (0, LANE - reps * half)))

    pq = np.zeros((2 * LANE, MLA_HEADS * LANE), np.float32)
    pk = np.zeros((2 * LANE, MLA_HEADS * LANE), np.float32)
    vones = np.zeros((1, MLA_HEADS * LANE), np.float32)
    for h in range(MLA_HEADS):
        for i in range(half):
            pq[h * half + i, h * LANE + MLA_NOPE + i] = 1.0
            pq[LANE + h * half + i, h * LANE + MLA_NOPE + half + i] = 1.0
            pk[i, h * LANE + MLA_NOPE + i] = 1.0
            pk[LANE + i, h * LANE + MLA_NOPE + half + i] = 1.0
        vones[0, h * LANE + (MLA_V if h % 2 == 0 else 0)] = 1.0
    return cos, sin, jnp.asarray(pq, BF16), jnp.asarray(pk, BF16), jnp.asarray(vones)


def _pre_mixer(x2, g, win_p, b_glu, q_norm, wuq_p, kv_norm, wukv_p, tables, seq, tm):
    n, d = x2.shape
    cos, sin, pq, pk, vones = tables
    conv_ch = b_glu.shape[-1] // 2
    hp = MLA_HEADS * LANE
    dd = DIFF_HEADS * DIFF_V
    nseq = seq // tm
    full = lambda a: pl.BlockSpec(a.shape, lambda i: (0,) * a.ndim)
    tok = lambda w: pl.BlockSpec((tm, w), lambda i: (i, 0))
    pos = pl.BlockSpec((tm, LANE), lambda i: (i % nseq, 0))
    args = (x2, g, win_p, b_glu, q_norm, wuq_p, kv_norm, wukv_p, vones, pq, pk)
    kern = functools.partial(_pre_mixer_kernel, mla_scale=(MLA_NOPE + MLA_ROPE) ** -0.5,
                             diff_scale=DIFF_HD ** -0.5, conv_ch=conv_ch)
    return pl.pallas_call(
        kern,
        grid=(n // tm,),
        in_specs=[tok(d)] + [full(a) for a in args[1:]] + [pos, pos],
        out_specs=[tok(conv_ch), tok(hp), tok(hp), tok(hp), tok(dd), tok(dd), tok(dd)],
        out_shape=[jax.ShapeDtypeStruct((n, conv_ch), F32)] + [jax.ShapeDtypeStruct((n, hp), BF16)] * 3
        + [jax.ShapeDtypeStruct((n, dd), BF16)] * 3,
        compiler_params=_params("parallel"),
        name="pre_mixer",
    )(*args, cos, sin)


CONV_PAD = 32


def _conv_kernel(u_ref, wdw_ref, bdw_ref, lng_ref, lnb_ref, wpw_ref, bpw_ref, o_ref, upad_ref, *, width, rows):
    seq, ch = u_ref.shape[1], u_ref.shape[2]
    upad_ref[0:CONV_PAD, :] = jnp.zeros((CONV_PAD, ch), F32)
    upad_ref[CONV_PAD:, :] = u_ref[0]
    first = CONV_PAD - (width - 1)
    for c in range(seq // rows):
        acc = jnp.broadcast_to(bdw_ref[...], (rows, ch))
        for j in range(width):
            acc = acc + wdw_ref[j:j + 1, :] * upad_ref[pl.ds(c * rows + first + j, rows), :]
        mu = jnp.mean(acc, axis=-1, keepdims=True)
        cen = acc - mu
        var = jnp.mean(cen * cen, axis=-1, keepdims=True)
        y = cen * lax.rsqrt(var + LN_EPS) * lng_ref[...] + lnb_ref[...]
        y = y * jax.nn.sigmoid(y)
        o_ref[0, c * rows:(c + 1) * rows, :] = (_dot(y.astype(BF16), wpw_ref[...]) + bpw_ref[...]).astype(BF16)


def _conv_module(u3, w_dw, b_dw, ln_g, ln_b, w_pw, b_pw, rows):
    b, seq, ch = u3.shape
    width = w_dw.shape[0]
    assert width - 1 <= CONV_PAD
    full = lambda a: pl.BlockSpec(a.shape, lambda i: (0,) * a.ndim)
    args = (w_dw, b_dw, ln_g, ln_b, w_pw, b_pw)
    return pl.pallas_call(
        functools.partial(_conv_kernel, width=width, rows=rows),
        grid=(b,),
        in_specs=[pl.BlockSpec((1, seq, ch), lambda i: (i, 0, 0))] + [full(a) for a in args],
        out_specs=pl.BlockSpec((1, seq, ch), lambda i: (i, 0, 0)),
        out_shape=jax.ShapeDtypeStruct((b, seq, ch), BF16),
        scratch_shapes=[pltpu.VMEM((seq + CONV_PAD, ch), F32)],
        compiler_params=_params("parallel"),
        name="conv_module",
    )(u3, *args)


def _mla_kernel(q_ref, k_ref, v_ref, o_ref, acc_ref, m_ref, *, tq):
    i = pl.program_id(2)
    row = lax.broadcasted_iota(jnp.int32, (tq, tq), 0)
    col = lax.broadcasted_iota(jnp.int32, (tq, tq), 1)
    lane = lax.broadcasted_iota(jnp.int32, (tq, LANE), 1)
    outs = []
    for sub in range(2):
        lanes = slice(sub * LANE, (sub + 1) * LANE)
        q = q_ref[0, :, lanes]
        m_ref[...] = jnp.full((tq, 1), NEG_BIG, F32)
        acc_ref[...] = jnp.zeros((tq, LANE), F32)

        def step(j, masked):
            start = pl.multiple_of(j * tq, tq)
            k = k_ref[0, pl.ds(start, tq), lanes]
            v = v_ref[0, pl.ds(start, tq), lanes]
            s = _dot_nt(q, k)
            if masked:
                s = jnp.where(row >= col, s, NEG_BIG)
            m_old = m_ref[...]
            m_new = jnp.maximum(m_old, jnp.max(s, axis=-1, keepdims=True))
            p = jnp.exp(s - m_new)
            acc_ref[...] = jnp.exp(m_old - m_new) * acc_ref[...] + _dot(p.astype(BF16), v)
            m_ref[...] = m_new

        def body(j, carry):
            step(j, False)
            return carry

        lax.fori_loop(0, i, body, 0)
        step(i, True)
        acc = acc_ref[...]
        den = acc[:, MLA_V:MLA_V + 1] if sub == 0 else acc[:, 0:1]
        outs.append(acc / den)
    o_ref[0] = jnp.where(lane < MLA_V, outs[0], outs[1]).astype(BF16)


def _mla_attention(q3, k3, v3, tq):
    b, seq, _ = q3.shape
    return pl.pallas_call(
        functools.partial(_mla_kernel, tq=tq),
        grid=(b, HEAD_PAIRS, seq // tq),
        in_specs=[pl.BlockSpec((1, tq, 2 * LANE), lambda bi, hp, i: (bi, i, hp)),
                  pl.BlockSpec((1, seq, 2 * LANE), lambda bi, hp, i: (bi, 0, hp)),
                  pl.BlockSpec((1, seq, 2 * LANE), lambda bi, hp, i: (bi, 0, hp))],
        out_specs=pl.BlockSpec((1, tq, LANE), lambda bi, hp, i: (bi, i, hp)),
        out_shape=jax.ShapeDtypeStruct((b, seq, HEAD_PAIRS * LANE), BF16),
        scratch_shapes=[pltpu.VMEM((tq, LANE), F32), pltpu.VMEM((tq, 1), F32)],
        compiler_params=_params("parallel", "parallel", "arbitrary"),
        name="mla_attention",
    )(q3, k3, v3)


def _diff_kernel(lq1_ref, lk1_ref, lq2_ref, lk2_ref, subln_ref, slope_ref, q_ref, k_ref, v_ref, o_ref,
                 acc_ref, m_ref, l_ref, *, tq, lam_init):
    i = pl.program_id(2)
    row = lax.broadcasted_iota(jnp.int32, (tq, tq), 0)
    col = lax.broadcasted_iota(jnp.int32, (tq, tq), 1)
    lane = lax.broadcasted_iota(jnp.int32, (tq, LANE), 1)
    colf = lax.broadcasted_iota(jnp.int32, (1, tq), 1).astype(F32)
    lam = (jnp.exp(jnp.sum(lq1_ref[...] * lk1_ref[...], axis=-1, keepdims=True))
           - jnp.exp(jnp.sum(lq2_ref[...] * lk2_ref[...], axis=-1, keepdims=True)) + lam_init)
    qpair = q_ref[0]
    zero = jnp.zeros_like(qpair)
    outs = []
    for sub in range(2):
        slope = jnp.concatenate([slope_ref[0, sub:sub + 1, :]] * (tq // LANE), axis=-1)
        for mp in range(2):
            idx = 2 * sub + mp
            off = sub * DIFF_V + mp * DIFF_HD
            q = jnp.where((lane >= off) & (lane < off + DIFF_HD), qpair, zero)
            m_ref[idx] = jnp.full((tq, 1), NEG_BIG, F32)
            l_ref[idx] = jnp.zeros((tq, 1), F32)
            acc_ref[idx] = jnp.zeros((tq, LANE), F32)

            def step(j, masked, q=q, idx=idx, slope=slope):
                start = pl.multiple_of(j * tq, tq)
                k = k_ref[0, pl.ds(start, tq), :]
                v = v_ref[0, pl.ds(start, tq), :]
                bias = slope * (colf + ((j - i) * tq).astype(F32))
                s = _dot_nt(q, k) + bias
                if masked:
                    s = jnp.where(row >= col, s, NEG_BIG)
                m_old = m_ref[idx]
                m_new = jnp.maximum(m_old, jnp.max(s, axis=-1, keepdims=True))
                p = jnp.exp(s - m_new)
                alpha = jnp.exp(m_old - m_new)
                l_ref[idx] = alpha * l_ref[idx] + jnp.sum(p, axis=-1, keepdims=True)
                acc_ref[idx] = alpha * acc_ref[idx] + _dot(p.astype(BF16), v)
                m_ref[idx] = m_new

            def body(j, carry, step=step):
                step(j, False)
                return carry

            lax.fori_loop(0, i, body, 0)
            step(i, True)
        a1, a2 = 2 * sub, 2 * sub + 1
        outs.append(acc_ref[a1] / l_ref[a1] - lam * (acc_ref[a2] / l_ref[a2]))
    left = lane < DIFF_V
    o = jnp.where(left, outs[0], outs[1])
    sq = o * o
    ss_l = jnp.sum(jnp.where(left, sq, 0.0), axis=-1, keepdims=True)
    ss_r = jnp.sum(jnp.where(left, 0.0, sq), axis=-1, keepdims=True)
    ms = jnp.where(left, ss_l, ss_r) * (1.0 / DIFF_V)
    o_ref[0] = (o * lax.rsqrt(ms + LN_EPS) * subln_ref[...] * (1.0 - lam_init)).astype(BF16)


def _diff_attention(lq1, lk1, lq2, lk2, subln2, slope_tab, q3, k3, v3, tq, lam_init):
    b, seq, _ = q3.shape
    small = lambda a: pl.BlockSpec(a.shape, lambda bi, hp, i: (0,) * a.ndim)
    return pl.pallas_call(
        functools.partial(_diff_kernel, tq=tq, lam_init=lam_init),
        grid=(b, HEAD_PAIRS, seq // tq),
        in_specs=[small(lq1), small(lk1), small(lq2), small(lk2), small(subln2),
                  pl.BlockSpec((1, 2, LANE), lambda bi, hp, i: (hp, 0, 0)),
                  pl.BlockSpec((1, tq, LANE), lambda bi, hp, i: (bi, i, hp)),
                  pl.BlockSpec((1, seq, LANE), lambda bi, hp, i: (bi, 0, hp)),
                  pl.BlockSpec((1, seq, LANE), lambda bi, hp, i: (bi, 0, hp))],
        out_specs=pl.BlockSpec((1, tq, LANE), lambda bi, hp, i: (bi, i, hp)),
        out_shape=jax.ShapeDtypeStruct((b, seq, HEAD_PAIRS * LANE), BF16),
        scratch_shapes=[pltpu.VMEM((4, tq, LANE), F32), pltpu.VMEM((4, tq, 1), F32), pltpu.VMEM((4, tq, 1), F32)],
        compiler_params=_params("parallel", "parallel", "arbitrary"),
        name="diff_attention",
    )(lq1, lk1, lq2, lk2, subln2, slope_tab, q3, k3, v3)


def _post_mixer_kernel(x_ref, yc_ref, ym_ref, yd_ref, wout_ref, g_ref, *rest, route):
    if route:
        wr_ref, x1_ref, h_ref, gate_ref, cat_ref = rest
    else:
        x1_ref, h_ref, cat_ref = rest
    c0 = yc_ref.shape[1]
    c1 = c0 + ym_ref.shape[1]
    cat_ref[:, 0:c0] = yc_ref[...]
    cat_ref[:, c0:c1] = ym_ref[...]
    cat_ref[:, c1:] = yd_ref[...]
    x1 = x_ref[...] + _dot(cat_ref[...], wout_ref[...])
    x1_ref[...] = x1
    h = _rms(x1, g_ref[...], RMS_EPS)
    h_ref[...] = h.astype(BF16)
    if route:
        n_exp = wr_ref.shape[0]
        tm = h.shape[0]
        lane = lax.broadcasted_iota(jnp.int32, (tm, LANE), 1)
        logits = jnp.full((tm, LANE), NEG_BIG, F32)
        for e in range(n_exp):
            le = jnp.sum(h * wr_ref[e:e + 1, :], axis=-1, keepdims=True)
            logits = jnp.where(lane == e, le, logits)
        m1 = jnp.max(logits, axis=-1, keepdims=True)
        i1 = jnp.min(jnp.where(logits == m1, lane, LANE), axis=-1, keepdims=True)
        rest_l = jnp.where(lane == i1, NEG_BIG, logits)
        m2 = jnp.max(rest_l, axis=-1, keepdims=True)
        i2 = jnp.min(jnp.where(rest_l == m2, lane, LANE), axis=-1, keepdims=True)
        e2 = jnp.exp(m2 - m1)
        w1 = 1.0 / (1.0 + e2)
        w2 = e2 / (1.0 + e2)
        gate_ref[...] = jnp.where(lane == i1, w1, 0.0) + jnp.where(lane == i2, w2, 0.0)


def _post_mixer(x2, yc, ym, yd, wout, g, w_router_t, tm):
    n, d = x2.shape
    route = w_router_t is not None
    tok = lambda w: pl.BlockSpec((tm, w), lambda i: (i, 0))
    full = lambda a: pl.BlockSpec(a.shape, lambda i: (0,) * a.ndim)
    dmix = wout.shape[0]
    in_specs = [tok(d), tok(yc.shape[1]), tok(ym.shape[1]), tok(yd.shape[1]), full(wout), full(g)]
    out_specs = [tok(d), tok(d)]
    out_shape = [jax.ShapeDtypeStruct((n, d), F32), jax.ShapeDtypeStruct((n, d), BF16)]
    args = [x2, yc, ym, yd, wout, g]
    if route:
        in_specs.append(full(w_router_t))
        out_specs.append(tok(LANE))
        out_shape.append(jax.ShapeDtypeStruct((n, LANE), F32))
        args.append(w_router_t)
    return pl.pallas_call(
        functools.partial(_post_mixer_kernel, route=route),
        grid=(n // tm,),
        in_specs=in_specs,
        out_specs=out_specs,
        out_shape=out_shape,
        scratch_shapes=[pltpu.VMEM((tm, dmix), BF16)],
        compiler_params=_params("parallel"),
        name="post_mixer_route" if route else "post_mixer",
    )(*args)


def _ffn_kernel(h_ref, gate_ref, wg_ref, wu_ref, wd_ref, x1_ref, g_ref, o_ref, acc_ref, *, gated, final):
    e = pl.program_id(1)
    f = pl.program_id(2)

    @pl.when((e == 0) & (f == 0))
    def _():
        acc_ref[...] = jnp.zeros_like(acc_ref)

    h = h_ref[...]
    a = _dot(h, wg_ref[0])
    b = _dot(h, wu_ref[0])
    t = a * jax.nn.sigmoid(a) * b
    if gated:
        lane = lax.broadcasted_iota(jnp.int32, gate_ref.shape, 1)
        t = t * jnp.sum(jnp.where(lane == e, gate_ref[...], 0.0), axis=-1, keepdims=True)
    acc_ref[...] += _dot(t.astype(BF16), wd_ref[0])

    @pl.when((e == pl.num_programs(1) - 1) & (f == pl.num_programs(2) - 1))
    def _():
        y = x1_ref[...] + acc_ref[...]
        o_ref[...] = _rms(y, g_ref[...], RMS_EPS) if final else y


def _ffn(h, gates, wg, wu, wd, x1, g, tm, tf, gated, final):
    n, d = x1.shape
    n_exp, _, dff = wg.shape
    return pl.pallas_call(
        functools.partial(_ffn_kernel, gated=gated, final=final),
        grid=(n // tm, n_exp, dff // tf),
        in_specs=[pl.BlockSpec((tm, d), lambda i, e, f: (i, 0)),
                  pl.BlockSpec((tm, gates.shape[1]), lambda i, e, f: (i, 0)),
                  pl.BlockSpec((1, d, tf), lambda i, e, f: (e, 0, f)),
                  pl.BlockSpec((1, d, tf), lambda i, e, f: (e, 0, f)),
                  pl.BlockSpec((1, tf, d), lambda i, e, f: (e, f, 0)),
                  pl.BlockSpec((tm, d), lambda i, e, f: (i, 0)),
                  pl.BlockSpec((1, d), lambda i, e, f: (0, 0))],
        out_specs=pl.BlockSpec((tm, d), lambda i, e, f: (i, 0)),
        out_shape=jax.ShapeDtypeStruct((n, d), F32),
        scratch_shapes=[pltpu.VMEM((tm, d), F32)],
        compiler_params=_params("parallel", "arbitrary", "arbitrary"),
        name="ffn_experts" if gated else "ffn_dense",
    )(h, gates, wg, wu, wd, x1, g)


def _pick(n, pref):
    t = min(n, pref)
    while n % t:
        t //= 2
    return t


def kernel(x, attn_norm, w_in, b_glu, w_dw, b_dw, conv_ln_g, conv_ln_b, w_pw, b_pw, q_norm, w_uq, kv_norm, w_ukv, lambda_q1, lambda_k1, lambda_q2, lambda_k2, diff_subln, w_out, ffn_norm, w_gate_dense, w_up_dense, w_down_dense, w_router, w_gate_exp, w_up_exp, w_down_exp, final_norm):
    bsz, seq, d = x.shape
    n = bsz * seq
    depth = attn_norm.shape[0]
    dff = w_gate_dense.shape[-1]
    tm = _pick(seq, 512)
    tq = _pick(seq, 256)
    tm_ffn = _pick(n, 1024)
    tf = 256 if dff % 256 == 0 else LANE
    conv_rows = _pick(seq, 256)
    row = lambda a: a.reshape(1, -1)

    tables = _static_tables(seq)
    slope_tab = jnp.broadcast_to(
        (2.0 ** (-8.0 * jnp.arange(1, DIFF_HEADS + 1, dtype=F32) / DIFF_HEADS)).reshape(HEAD_PAIRS, 2, 1),
        (HEAD_PAIRS, 2, LANE))
    ones_gate = jnp.ones((n, 1), F32)

    x2 = x.reshape(n, d)
    for l in range(depth):
        win_p, wuq_p, wukv_p = _prep_in_weights(w_in[l], w_uq[l], w_ukv[l])
        u, q, k, v, dq, dk, dv = _pre_mixer(x2, row(attn_norm[l]), win_p, row(b_glu[l]), row(q_norm[l]), wuq_p,
                                            row(kv_norm[l]), wukv_p, tables, seq, tm)
        r3 = lambda a: a.reshape(bsz, seq, a.shape[-1])
        yc = _conv_module(r3(u), w_dw[l], row(b_dw[l]), row(conv_ln_g[l]), row(conv_ln_b[l]),
                          w_pw[l].astype(BF16), row(b_pw[l]), conv_rows)
        ym = _mla_attention(r3(q), r3(k), r3(v), tq)
        lam_init = 0.8 - 0.6 * math.exp(-0.3 * l)
        subln2 = jnp.tile(row(diff_subln[l]), (1, LANE // DIFF_V))
        yd = _diff_attention(row(lambda_q1[l]), row(lambda_k1[l]), row(lambda_q2[l]), row(lambda_k2[l]), subln2,
                             slope_tab, r3(dq), r3(dk), r3(dv), tq, lam_init)
        flat = lambda a: a.reshape(n, a.shape[-1])
        final = l == depth - 1
        fin_g = row(final_norm) if final else row(ffn_norm[l])
        j = l // 2
        if l % 2 == 0:
            x1, h2 = _post_mixer(x2, flat(yc), flat(ym), flat(yd), w_out[l].astype(BF16), row(ffn_norm[l]), None, tm)
            x2 = _ffn(h2, ones_gate, w_gate_dense[j][None].astype(BF16), w_up_dense[j][None].astype(BF16),
                      w_down_dense[j][None].astype(BF16), x1, fin_g, tm_ffn, tf, gated=False, final=final)
        else:
            x1, h2, gates = _post_mixer(x2, flat(yc), flat(ym), flat(yd), w_out[l].astype(BF16), row(ffn_norm[l]),
                                        w_router[j].T, tm)
            x2 = _ffn(h2, gates, w_gate_exp[j].astype(BF16), w_up_exp[j].astype(BF16), w_down_exp[j].astype(BF16),
                      x1, fin_g, tm_ffn, tf, gated=True, final=final)
    return x2.reshape(bsz, seq, d)
```

```python
import functools
import math

import numpy as np
import jax
import jax.numpy as jnp
from jax import lax
from jax.experimental import pallas as pl
from jax.experimental.pallas import tpu as pltpu

MLA_HEADS = 6
MLA_NOPE = 64
MLA_ROPE = 32
MLA_V = 64
DIFF_HEADS = 6
DIFF_HD = 32
DIFF_V = 2 * DIFF_HD
TOP_K = 2
ROPE_THETA = 10000.0
RMS_EPS = 1e-6
LN_EPS = 1e-5

LANE = 128
NEG_BIG = -1e30
LOG2E = math.log2(math.e)
VMEM_LIMIT = 56 * 1024 * 1024

F32 = jnp.float32
BF16 = jnp.bfloat16

HEAD_PAIRS = MLA_HEADS // 2
assert MLA_HEADS == DIFF_HEADS and MLA_HEADS % 2 == 0
assert MLA_V == DIFF_V == LANE // 2 and MLA_NOPE + MLA_ROPE <= LANE


def _rms(x, g, eps):
    return x * lax.rsqrt(jnp.mean(x * x, axis=-1, keepdims=True) + eps) * g


def _dot(a, b):
    return jnp.dot(a, b, preferred_element_type=F32)


def _dot_nt(a, b):
    return lax.dot_general(a, b, (((1,), (1,)), ((), ())), preferred_element_type=F32)


def _params(*sem):
    return pltpu.CompilerParams(dimension_semantics=sem, vmem_limit_bytes=VMEM_LIMIT)


C_CONV = 0
C_CQ = 512
C_CKV = 768
C_KRA = 896
C_KRB = 1024
C_DQ = 1152
C_DK = 1536
C_DV = 1920
C_END = 2304


def _pre_mixer_kernel(x_ref, g_ref, win_ref, bglu_ref, qn_ref, wuq_ref, kvn_ref, wukv_ref, vones_ref,
                      pq_ref, pk_ref, cos_ref, sin_ref,
                      u_ref, q_ref, k_ref, v_ref, dq_ref, dk_ref, dv_ref, *, mla_scale, diff_scale, conv_ch):
    h = _rms(x_ref[...], g_ref[...], RMS_EPS).astype(BF16)
    proj = _dot(h, win_ref[...])

    a = proj[:, C_CONV:C_CONV + 2 * conv_ch] + bglu_ref[...]
    u_ref[...] = a[:, :conv_ch] * jax.nn.sigmoid(a[:, conv_ch:])

    dq_ref[...] = (proj[:, C_DQ:C_DK] * diff_scale).astype(BF16)
    dk_ref[...] = proj[:, C_DK:C_DV].astype(BF16)
    dv_ref[...] = proj[:, C_DV:C_END].astype(BF16)

    cos = cos_ref[...]
    sin = sin_ref[...]

    hq = _rms(proj[:, C_CQ:C_CKV], qn_ref[...], RMS_EPS).astype(BF16)
    qall = _dot(hq, wuq_ref[...])
    npad = MLA_HEADS * LANE
    r1 = qall[:, npad:npad + LANE]
    r2 = qall[:, npad + LANE:npad + 2 * LANE]
    qrot = (jnp.concatenate([r1 * cos - r2 * sin, r1 * sin + r2 * cos], axis=-1) * mla_scale).astype(BF16)
    q_ref[...] = (qall[:, :npad] * mla_scale + _dot(qrot, pq_ref[...])).astype(BF16)

    hkv = _rms(proj[:, C_CKV:C_KRA], kvn_ref[...], RMS_EPS).astype(BF16)
    kv = _dot(hkv, wukv_ref[...])
    k1 = proj[:, C_KRA:C_KRB]
    k2 = proj[:, C_KRB:C_DQ]
    krot = jnp.concatenate([k1 * cos - k2 * sin, k1 * sin + k2 * cos], axis=-1).astype(BF16)
    k_ref[...] = (kv[:, :npad] + _dot(krot, pk_ref[...])).astype(BF16)
    v_ref[...] = (kv[:, npad:] + vones_ref[...]).astype(BF16)


def _prep_in_weights(w_in, w_uq, w_ukv):
    d = w_in.shape[0]
    cuts = np.cumsum([512, 256, 128, 32, 384, 384])
    conv, cq, ckv, kr, dq, dk, dv = jnp.split(w_in, [int(c) for c in cuts], axis=-1)
    half = MLA_ROPE // 2
    zpad = jnp.zeros((d, LANE - half), w_in.dtype)
    win_p = jnp.concatenate([conv, cq, ckv, kr[:, :half], zpad, kr[:, half:], zpad, dq, dk, dv], axis=-1)
    assert win_p.shape[1] == C_END

    qr = MLA_NOPE + MLA_ROPE
    rq = w_uq.shape[0]
    wq = w_uq.reshape(rq, MLA_HEADS, qr)
    nope = jnp.pad(wq[:, :, :MLA_NOPE], ((0, 0), (0, 0), (0, LANE - MLA_NOPE))).reshape(rq, MLA_HEADS * LANE)
    r1 = jnp.pad(wq[:, :, MLA_NOPE:MLA_NOPE + half].reshape(rq, MLA_HEADS * half), ((0, 0), (0, LANE - MLA_HEADS * half)))
    r2 = jnp.pad(wq[:, :, MLA_NOPE + half:].reshape(rq, MLA_HEADS * half), ((0, 0), (0, LANE - MLA_HEADS * half)))
    wuq_p = jnp.concatenate([nope, r1, r2], axis=-1)

    rkv = w_ukv.shape[0]
    wkv = w_ukv.reshape(rkv, MLA_HEADS, MLA_NOPE + MLA_V)
    kn = jnp.pad(wkv[:, :, :MLA_NOPE], ((0, 0), (0, 0), (0, LANE - MLA_NOPE)))
    vv = wkv[:, :, MLA_NOPE:]
    v_even = jnp.pad(vv, ((0, 0), (0, 0), (0, LANE - MLA_V)))
    v_odd = jnp.pad(vv, ((0, 0), (0, 0), (LANE - MLA_V, 0)))
    odd = (jnp.arange(MLA_HEADS) % 2 == 1)[None, :, None]
    vp = jnp.where(odd, v_odd, v_even)
    wukv_p = jnp.concatenate([kn.reshape(rkv, -1), vp.reshape(rkv, -1)], axis=-1)
    return win_p.astype(BF16), wuq_p.astype(BF16), wukv_p.astype(BF16)


def _static_tables(seq):
    half = MLA_ROPE // 2
    freq = ROPE_THETA ** (-jnp.arange(half, dtype=F32) / half)
    ang = jnp.arange(seq, dtype=F32)[:, None] * freq[None, :]
    reps = MLA_HEADS
    cos = jnp.pad(jnp.tile(jnp.cos(ang), (1, reps)), ((0, 0), (0, LANE - reps * half)))
    sin = jnp.pad(jnp.tile(jnp.sin(ang), (1, reps)), ((0, 0), (0, LANE - reps * half)))

    pq = np.zeros((2 * LANE, MLA_HEADS * LANE), np.float32)
    pk = np.zeros((2 * LANE, MLA_HEADS * LANE), np.float32)
    vones = np.zeros((1, MLA_HEADS * LANE), np.float32)
    for h in range(MLA_HEADS):
        for i in range(half):
            pq[h * half + i, h * LANE + MLA_NOPE + i] = 1.0
            pq[LANE + h * half + i, h * LANE + MLA_NOPE + half + i] = 1.0
            pk[i, h * LANE + MLA_NOPE + i] = 1.0
            pk[LANE + i, h * LANE + MLA_NOPE + half + i] = 1.0
        vones[0, h * LANE + (MLA_V if h % 2 == 0 else 0)] = 1.0
    return cos, sin, jnp.asarray(pq, BF16), jnp.asarray(pk, BF16), jnp.asarray(vones)


def _pre_mixer(x2, g, win_p, b_glu, q_norm, wuq_p, kv_norm, wukv_p, tables, seq, tm):
    n, d = x2.shape
    cos, sin, pq, pk, vones = tables
    conv_ch = b_glu.shape[-1] // 2
    hp = MLA_HEADS * LANE
    dd = DIFF_HEADS * DIFF_V
    nseq = seq // tm
    full = lambda a: pl.BlockSpec(a.shape, lambda i: (0,) * a.ndim)
    tok = lambda w: pl.BlockSpec((tm, w), lambda i: (i, 0))
    pos = pl.BlockSpec((tm, LANE), lambda i: (i % nseq, 0))
    args = (x2, g, win_p, b_glu, q_norm, wuq_p, kv_norm, wukv_p, vones, pq, pk)
    kern = functools.partial(_pre_mixer_kernel, mla_scale=LOG2E * (MLA_NOPE + MLA_ROPE) ** -0.5,
                             diff_scale=LOG2E * DIFF_HD ** -0.5, conv_ch=conv_ch)
    return pl.pallas_call(
        kern,
        grid=(n // tm,),
        in_specs=[tok(d)] + [full(a) for a in args[1:]] + [pos, pos],
        out_specs=[tok(conv_ch), tok(hp), tok(hp), tok(hp), tok(dd), tok(dd), tok(dd)],
        out_shape=[jax.ShapeDtypeStruct((n, conv_ch), F32)] + [jax.ShapeDtypeStruct((n, hp), BF16)] * 3
        + [jax.ShapeDtypeStruct((n, dd), BF16)] * 3,
        compiler_params=_params("parallel"),
        name="pre_mixer",
    )(*args, cos, sin)


CONV_PAD = 32


def _conv_kernel(u_ref, wdw_ref, bdw_ref, lng_ref, lnb_ref, wpw_ref, bpw_ref, o_ref, upad_ref, *, width, rows):
    seq, ch = u_ref.shape[1], u_ref.shape[2]
    upad_ref[0:CONV_PAD, :] = jnp.zeros((CONV_PAD, ch), F32)
    upad_ref[CONV_PAD:, :] = u_ref[0]
    first = CONV_PAD - (width - 1)
    for c in range(seq // rows):
        acc = jnp.broadcast_to(bdw_ref[...], (rows, ch))
        for j in range(width):
            acc = acc + wdw_ref[j:j + 1, :] * upad_ref[pl.ds(c * rows + first + j, rows), :]
        mu = jnp.mean(acc, axis=-1, keepdims=True)
        cen = acc - mu
        var = jnp.mean(cen * cen, axis=-1, keepdims=True)
        y = cen * lax.rsqrt(var + LN_EPS) * lng_ref[...] + lnb_ref[...]
        y = y * jax.nn.sigmoid(y)
        o_ref[0, c * rows:(c + 1) * rows, :] = (_dot(y.astype(BF16), wpw_ref[...]) + bpw_ref[...]).astype(BF16)


def _conv_module(u3, w_dw, b_dw, ln_g, ln_b, w_pw, b_pw, rows):
    b, seq, ch = u3.shape
    width = w_dw.shape[0]
    assert width - 1 <= CONV_PAD
    full = lambda a: pl.BlockSpec(a.shape, lambda i: (0,) * a.ndim)
    args = (w_dw, b_dw, ln_g, ln_b, w_pw, b_pw)
    return pl.pallas_call(
        functools.partial(_conv_kernel, width=width, rows=rows),
        grid=(b,),
        in_specs=[pl.BlockSpec((1, seq, ch), lambda i: (i, 0, 0))] + [full(a) for a in args],
        out_specs=pl.BlockSpec((1, seq, ch), lambda i: (i, 0, 0)),
        out_shape=jax.ShapeDtypeStruct((b, seq, ch), BF16),
        scratch_shapes=[pltpu.VMEM((seq + CONV_PAD, ch), F32)],
        compiler_params=_params("parallel"),
        name="conv_module",
    )(u3, *args)


def _softmax_step(chunks, v, idx, m_ref, acc_ref, l_ref):
    m_old = m_ref[idx]
    m_cur = functools.reduce(jnp.maximum, chunks)
    m_new = jnp.maximum(m_old, jnp.max(m_cur, axis=-1, keepdims=True))
    p = [jnp.exp2(c - m_new) for c in chunks]
    alpha = jnp.exp2(m_old - m_new)
    if l_ref is not None:
        l_ref[idx] = alpha * l_ref[idx] + functools.reduce(jnp.add, p)
    pb = jnp.concatenate([c.astype(BF16) for c in p], axis=-1)
    acc_ref[idx] = alpha * acc_ref[idx] + _dot(pb, v)
    m_ref[idx] = m_new


def _mla_kernel(q_ref, k_ref, v_ref, o_ref, acc_ref, m_ref, *, tq):
    nb = q_ref.shape[0]
    i = pl.program_id(1)
    nchunk = tq // LANE
    row = lax.broadcasted_iota(jnp.int32, (tq, LANE), 0)
    col = lax.broadcasted_iota(jnp.int32, (tq, LANE), 1)
    m_ref[...] = jnp.full(m_ref.shape, NEG_BIG, F32)
    acc_ref[...] = jnp.zeros(acc_ref.shape, F32)

    def step(j, masked):
        start = pl.multiple_of(j * tq, tq)
        for bb in range(nb):
            for h in range(MLA_HEADS):
                lanes = slice(h * LANE, (h + 1) * LANE)
                s = _dot_nt(q_ref[bb, :, lanes], k_ref[bb, pl.ds(start, tq), lanes])
                chunks = [s[:, c * LANE:(c + 1) * LANE] for c in range(nchunk)]
                if masked:
                    chunks = [jnp.where(row >= col + c * LANE, ch, NEG_BIG) for c, ch in enumerate(chunks)]
                _softmax_step(chunks, v_ref[bb, pl.ds(start, tq), lanes], bb * MLA_HEADS + h, m_ref, acc_ref, None)

    def body(j, carry):
        step(j, False)
        return carry

    lax.fori_loop(0, i, body, 0)
    step(i, True)
    for bb in range(nb):
        for hp in range(HEAD_PAIRS):
            a0 = acc_ref[bb * MLA_HEADS + 2 * hp]
            a1 = acc_ref[bb * MLA_HEADS + 2 * hp + 1]
            o0 = a0 / a0[:, MLA_V:MLA_V + 1]
            o1 = a1 / a1[:, 0:1]
            o_ref[bb, :, hp * LANE:(hp + 1) * LANE] = jnp.where(col < MLA_V, o0, o1).astype(BF16)


def _mla_attention(q3, k3, v3, tq, nb):
    b, seq, w = q3.shape
    stat = pltpu.VMEM((nb * MLA_HEADS, tq, LANE), F32)
    return pl.pallas_call(
        functools.partial(_mla_kernel, tq=tq),
        grid=(b // nb, seq // tq),
        in_specs=[pl.BlockSpec((nb, tq, w), lambda bi, i: (bi, i, 0)),
                  pl.BlockSpec((nb, seq, w), lambda bi, i: (bi, 0, 0)),
                  pl.BlockSpec((nb, seq, w), lambda bi, i: (bi, 0, 0))],
        out_specs=pl.BlockSpec((nb, tq, HEAD_PAIRS * LANE), lambda bi, i: (bi, i, 0)),
        out_shape=jax.ShapeDtypeStruct((b, seq, HEAD_PAIRS * LANE), BF16),
        scratch_shapes=[stat, stat],
        compiler_params=_params("parallel", "arbitrary"),
        name="mla_attention",
    )(q3, k3, v3)


N_DIFF_CHAINS = 2 * DIFF_HEADS


def _diff_kernel(lq1_ref, lk1_ref, lq2_ref, lk2_ref, subln_ref, slope_ref, q_ref, k_ref, v_ref, o_ref,
                 qm_ref, acc_ref, m_ref, l_ref, *, tq, lam_init):
    i = pl.program_id(1)
    nchunk = tq // LANE
    row = lax.broadcasted_iota(jnp.int32, (tq, LANE), 0)
    col = lax.broadcasted_iota(jnp.int32, (tq, LANE), 1)
    colf = lax.broadcasted_iota(jnp.int32, (1, LANE), 1).astype(F32)
    m_ref[...] = jnp.full(m_ref.shape, NEG_BIG, F32)
    l_ref[...] = jnp.zeros(l_ref.shape, F32)
    acc_ref[...] = jnp.zeros(acc_ref.shape, F32)
    for hp in range(HEAD_PAIRS):
        qpair = q_ref[0, :, hp * LANE:(hp + 1) * LANE]
        for c in range(4):
            keep = (col >= c * DIFF_HD) & (col < (c + 1) * DIFF_HD)
            qm_ref[4 * hp + c] = jnp.where(keep, qpair, jnp.zeros_like(qpair))

    def step(j, masked):
        start = pl.multiple_of(j * tq, tq)
        shift = (jnp.zeros((1, LANE), jnp.int32) + (j - i) * tq).astype(F32)
        for hp in range(HEAD_PAIRS):
            lanes = slice(hp * LANE, (hp + 1) * LANE)
            k = k_ref[0, pl.ds(start, tq), lanes]
            v = v_ref[0, pl.ds(start, tq), lanes]
            for c in range(4):
                h = 2 * hp + c // 2
                slope = slope_ref[h:h + 1, :]
                s = _dot_nt(qm_ref[4 * hp + c], k)
                chunks = [s[:, cc * LANE:(cc + 1) * LANE] + slope * (colf + (shift + float(cc * LANE)))
                          for cc in range(nchunk)]
                if masked:
                    chunks = [jnp.where(row >= col + cc * LANE, ch, NEG_BIG) for cc, ch in enumerate(chunks)]
                _softmax_step(chunks, v, 4 * hp + c, m_ref, acc_ref, l_ref)

    def body(j, carry):
        step(j, False)
        return carry

    lax.fori_loop(0, i, body, 0)
    step(i, True)

    lam = (jnp.exp(jnp.sum(lq1_ref[...] * lk1_ref[...], axis=-1, keepdims=True))
           - jnp.exp(jnp.sum(lq2_ref[...] * lk2_ref[...], axis=-1, keepdims=True)) + lam_init)
    left = col < DIFF_V
    for hp in range(HEAD_PAIRS):
        outs = []
        for sub in range(2):
            a1 = 4 * hp + 2 * sub
            a2 = a1 + 1
            l1 = jnp.sum(l_ref[a1], axis=-1, keepdims=True)
            l2 = jnp.sum(l_ref[a2], axis=-1, keepdims=True)
            outs.append(acc_ref[a1] / l1 - lam * (acc_ref[a2] / l2))
        o = jnp.where(left, outs[0], outs[1])
        sq = o * o
        ss_l = jnp.sum(jnp.where(left, sq, 0.0), axis=-1, keepdims=True)
        ss_r = jnp.sum(jnp.where(left, 0.0, sq), axis=-1, keepdims=True)
        ms = jnp.where(left, ss_l, ss_r) * (1.0 / DIFF_V)
        o_ref[0, :, hp * LANE:(hp + 1) * LANE] = (
            o * lax.rsqrt(ms + LN_EPS) * subln_ref[...] * (1.0 - lam_init)).astype(BF16)


def _diff_attention(lq1, lk1, lq2, lk2, subln2, slope_tab, q3, k3, v3, tq, lam_init):
    b, seq, w = q3.shape
    small = lambda a: pl.BlockSpec(a.shape, lambda bi, i: (0,) * a.ndim)
    stat = pltpu.VMEM((N_DIFF_CHAINS, tq, LANE), F32)
    return pl.pallas_call(
        functools.partial(_diff_kernel, tq=tq, lam_init=lam_init),
        grid=(b, seq // tq),
        in_specs=[small(lq1), small(lk1), small(lq2), small(lk2), small(subln2), small(slope_tab),
                  pl.BlockSpec((1, tq, w), lambda bi, i: (bi, i, 0)),
                  pl.BlockSpec((1, seq, w), lambda bi, i: (bi, 0, 0)),
                  pl.BlockSpec((1, seq, w), lambda bi, i: (bi, 0, 0))],
        out_specs=pl.BlockSpec((1, tq, w), lambda bi, i: (bi, i, 0)),
        out_shape=jax.ShapeDtypeStruct((b, seq, w), BF16),
        scratch_shapes=[pltpu.VMEM((N_DIFF_CHAINS, tq, LANE), BF16), stat, stat, stat],
        compiler_params=_params("parallel", "arbitrary"),
        name="diff_attention",
    )(lq1, lk1, lq2, lk2, subln2, slope_tab, q3, k3, v3)


def _post_mixer_kernel(x_ref, yc_ref, ym_ref, yd_ref, wout_ref, g_ref, *rest, route):
    if route:
        wr_ref, x1_ref, h_ref, gate_ref, cat_ref = rest
    else:
        x1_ref, h_ref, cat_ref = rest
    c0 = yc_ref.shape[1]
    c1 = c0 + ym_ref.shape[1]
    cat_ref[:, 0:c0] = yc_ref[...]
    cat_ref[:, c0:c1] = ym_ref[...]
    cat_ref[:, c1:] = yd_ref[...]
    x1 = x_ref[...] + _dot(cat_ref[...], wout_ref[...])
    x1_ref[...] = x1
    h = _rms(x1, g_ref[...], RMS_EPS)
    h_ref[...] = h.astype(BF16)
    if route:
        n_exp = wr_ref.shape[0]
        tm = h.shape[0]
        lane = lax.broadcasted_iota(jnp.int32, (tm, LANE), 1)
        logits = jnp.full((tm, LANE), NEG_BIG, F32)
        for e in range(n_exp):
            le = jnp.sum(h * wr_ref[e:e + 1, :], axis=-1, keepdims=True)
            logits = jnp.where(lane == e, le, logits)
        m1 = jnp.max(logits, axis=-1, keepdims=True)
        i1 = jnp.min(jnp.where(logits == m1, lane, LANE), axis=-1, keepdims=True)
        rest_l = jnp.where(lane == i1, NEG_BIG, logits)
        m2 = jnp.max(rest_l, axis=-1, keepdims=True)
        i2 = jnp.min(jnp.where(rest_l == m2, lane, LANE), axis=-1, keepdims=True)
        e2 = jnp.exp(m2 - m1)
        w1 = 1.0 / (1.0 + e2)
        w2 = e2 / (1.0 + e2)
        gate_ref[...] = jnp.where(lane == i1, w1, 0.0) + jnp.where(lane == i2, w2, 0.0)


def _post_mixer(x2, yc, ym, yd, wout, g, w_router_t, tm):
    n, d = x2.shape
    route = w_router_t is not None
    tok = lambda w: pl.BlockSpec((tm, w), lambda i: (i, 0))
    full = lambda a: pl.BlockSpec(a.shape, lambda i: (0,) * a.ndim)
    dmix = wout.shape[0]
    in_specs = [tok(d), tok(yc.shape[1]), tok(ym.shape[1]), tok(yd.shape[1]), full(wout), full(g)]
    out_specs = [tok(d), tok(d)]
    out_shape = [jax.ShapeDtypeStruct((n, d), F32), jax.ShapeDtypeStruct((n, d), BF16)]
    args = [x2, yc, ym, yd, wout, g]
    if route:
        in_specs.append(full(w_router_t))
        out_specs.append(tok(LANE))
        out_shape.append(jax.ShapeDtypeStruct((n, LANE), F32))
        args.append(w_router_t)
    return pl.pallas_call(
        functools.partial(_post_mixer_kernel, route=route),
        grid=(n // tm,),
        in_specs=in_specs,
        out_specs=out_specs,
        out_shape=out_shape,
        scratch_shapes=[pltpu.VMEM((tm, dmix), BF16)],
        compiler_params=_params("parallel"),
        name="post_mixer_route" if route else "post_mixer",
    )(*args)


def _ffn_kernel(h_ref, gate_ref, wg_ref, wu_ref, wd_ref, x1_ref, g_ref, o_ref, acc_ref, *, gated, final):
    e = pl.program_id(1)
    f = pl.program_id(2)

    @pl.when((e == 0) & (f == 0))
    def _():
        acc_ref[...] = jnp.zeros_like(acc_ref)

    h = h_ref[...]
    a = _dot(h, wg_ref[0])
    b = _dot(h, wu_ref[0])
    t = a * jax.nn.sigmoid(a) * b
    if gated:
        lane = lax.broadcasted_iota(jnp.int32, gate_ref.shape, 1)
        t = t * jnp.sum(jnp.where(lane == e, gate_ref[...], 0.0), axis=-1, keepdims=True)
    acc_ref[...] += _dot(t.astype(BF16), wd_ref[0])

    @pl.when((e == pl.num_programs(1) - 1) & (f == pl.num_programs(2) - 1))
    def _():
        y = x1_ref[...] + acc_ref[...]
        o_ref[...] = _rms(y, g_ref[...], RMS_EPS) if final else y


def _ffn(h, gates, wg, wu, wd, x1, g, tm, tf, gated, final):
    n, d = x1.shape
    n_exp, _, dff = wg.shape
    return pl.pallas_call(
        functools.partial(_ffn_kernel, gated=gated, final=final),
        grid=(n // tm, n_exp, dff // tf),
        in_specs=[pl.BlockSpec((tm, d), lambda i, e, f: (i, 0)),
                  pl.BlockSpec((tm, gates.shape[1]), lambda i, e, f: (i, 0)),
                  pl.BlockSpec((1, d, tf), lambda i, e, f: (e, 0, f)),
                  pl.BlockSpec((1, d, tf), lambda i, e, f: (e, 0, f)),
                  pl.BlockSpec((1, tf, d), lambda i, e, f: (e, f, 0)),
                  pl.BlockSpec((tm, d), lambda i, e, f: (i, 0)),
                  pl.BlockSpec((1, d), lambda i, e, f: (0, 0))],
        out_specs=pl.BlockSpec((tm, d), lambda i, e, f: (i, 0)),
        out_shape=jax.ShapeDtypeStruct((n, d), F32),
        scratch_shapes=[pltpu.VMEM((tm, d), F32)],
        compiler_params=_params("parallel", "arbitrary", "arbitrary"),
        name="ffn_experts" if gated else "ffn_dense",
    )(h, gates, wg, wu, wd, x1, g)


def _pick(n, pref):
    t = min(n, pref)
    while n % t:
        t //= 2
    return t


def kernel(x, attn_norm, w_in, b_glu, w_dw, b_dw, conv_ln_g, conv_ln_b, w_pw, b_pw, q_norm, w_uq, kv_norm, w_ukv, lambda_q1, lambda_k1, lambda_q2, lambda_k2, diff_subln, w_out, ffn_norm, w_gate_dense, w_up_dense, w_down_dense, w_router, w_gate_exp, w_up_exp, w_down_exp, final_norm):
    bsz, seq, d = x.shape
    n = bsz * seq
    depth = attn_norm.shape[0]
    dff = w_gate_dense.shape[-1]
    tm = _pick(seq, 512)
    tq = _pick(seq, 256)
    tm_ffn = _pick(n, 1024)
    tf = 256 if dff % 256 == 0 else LANE
    conv_rows = _pick(seq, 256)
    row = lambda a: a.reshape(1, -1)

    tables = _static_tables(seq)
    slope_tab = jnp.broadcast_to(
        (LOG2E * 2.0 ** (-8.0 * jnp.arange(1, DIFF_HEADS + 1, dtype=F32) / DIFF_HEADS))[:, None], (DIFF_HEADS, LANE))
    ones_gate = jnp.ones((n, 1), F32)

    x2 = x.reshape(n, d)
    for l in range(depth):
        win_p, wuq_p, wukv_p = _prep_in_weights(w_in[l], w_uq[l], w_ukv[l])
        u, q, k, v, dq, dk, dv = _pre_mixer(x2, row(attn_norm[l]), win_p, row(b_glu[l]), row(q_norm[l]), wuq_p,
                                            row(kv_norm[l]), wukv_p, tables, seq, tm)
        r3 = lambda a: a.reshape(bsz, seq, a.shape[-1])
        yc = _conv_module(r3(u), w_dw[l], row(b_dw[l]), row(conv_ln_g[l]), row(conv_ln_b[l]),
                          w_pw[l].astype(BF16), row(b_pw[l]), conv_rows)
        ym = _mla_attention(r3(q), r3(k), r3(v), tq, 2 if bsz % 2 == 0 else 1)
        lam_init = 0.8 - 0.6 * math.exp(-0.3 * l)
        subln2 = jnp.tile(row(diff_subln[l]), (1, LANE // DIFF_V))
        yd = _diff_attention(row(lambda_q1[l]), row(lambda_k1[l]), row(lambda_q2[l]), row(lambda_k2[l]), subln2,
                             slope_tab, r3(dq), r3(dk), r3(dv), tq, lam_init)
        flat = lambda a: a.reshape(n, a.shape[-1])
        final = l == depth - 1
        fin_g = row(final_norm) if final else row(ffn_norm[l])
        j = l // 2
        if l % 2 == 0:
            x1, h2 = _post_mixer(x2, flat(yc), flat(ym), flat(yd), w_out[l].astype(BF16), row(ffn_norm[l]), None, tm)
            x2 = _ffn(h2, ones_gate, w_gate_dense[j][None].astype(BF16), w_up_dense[j][None].astype(BF16),
                      w_down_dense[j][None].astype(BF16), x1, fin_g, tm_ffn, tf, gated=False, final=final)
        else:
            x1, h2, gates = _post_mixer(x2, flat(yc), flat(ym), flat(yd), w_out[l].astype(BF16), row(ffn_norm[l]),
                                        w_router[j].T, tm)
            x2 = _ffn(h2, gates, w_gate_exp[j].astype(BF16), w_up_exp[j].astype(BF16), w_down_exp[j].astype(BF16),
                      x1, fin_g, tm_ffn, tf, gated=True, final=final)
    return x2.reshape(bsz, seq, d)
```

```python
import functools
import math

import numpy as np
import jax
import jax.numpy as jnp
from jax import lax
from jax.experimental import pallas as pl
from jax.experimental.pallas import tpu as pltpu

MLA_HEADS = 6
MLA_NOPE = 64
MLA_ROPE = 32
MLA_V = 64
DIFF_HEADS = 6
DIFF_HD = 32
DIFF_V = 2 * DIFF_HD
TOP_K = 2
ROPE_THETA = 10000.0
RMS_EPS = 1e-6
LN_EPS = 1e-5

LANE = 128
NEG_BIG = -1e30
LOG2E = math.log2(math.e)
VMEM_LIMIT = 56 * 1024 * 1024

F32 = jnp.float32
BF16 = jnp.bfloat16

HEAD_PAIRS = MLA_HEADS // 2
assert MLA_HEADS == DIFF_HEADS and MLA_HEADS % 2 == 0
assert MLA_V == DIFF_V == LANE // 2 and MLA_NOPE + MLA_ROPE <= LANE


def _rms(x, g, eps):
    return x * lax.rsqrt(jnp.mean(x * x, axis=-1, keepdims=True) + eps) * g


def _dot(a, b):
    return jnp.dot(a, b, preferred_element_type=F32)


def _dot_nt(a, b):
    return lax.dot_general(a, b, (((1,), (1,)), ((), ())), preferred_element_type=F32)


def _params(*sem):
    return pltpu.CompilerParams(dimension_semantics=sem, vmem_limit_bytes=VMEM_LIMIT)


C_CONV = 0
C_CQ = 512
C_CKV = 768
C_KRA = 896
C_KRB = 1024
C_DQ = 1152
C_DK = 1536
C_DV = 1920
C_END = 2304


def _pre_mixer_kernel(x_ref, g_ref, win_ref, bglu_ref, qn_ref, wuq_ref, kvn_ref, wukv_ref, vones_ref,
                      pq_ref, pk_ref, cos_ref, sin_ref,
                      u_ref, q_ref, k_ref, v_ref, dq_ref, dk_ref, dv_ref, *, mla_scale, diff_scale, conv_ch):
    h = _rms(x_ref[...], g_ref[...], RMS_EPS).astype(BF16)
    proj = _dot(h, win_ref[...])

    a = proj[:, C_CONV:C_CONV + 2 * conv_ch] + bglu_ref[...]
    u_ref[...] = a[:, :conv_ch] * jax.nn.sigmoid(a[:, conv_ch:])

    dq_ref[...] = (proj[:, C_DQ:C_DK] * diff_scale).astype(BF16)
    dk_ref[...] = proj[:, C_DK:C_DV].astype(BF16)
    dv_ref[...] = proj[:, C_DV:C_END].astype(BF16)

    cos = cos_ref[...]
    sin = sin_ref[...]

    hq = _rms(proj[:, C_CQ:C_CKV], qn_ref[...], RMS_EPS).astype(BF16)
    qall = _dot(hq, wuq_ref[...])
    npad = MLA_HEADS * LANE
    r1 = qall[:, npad:npad + LANE]
    r2 = qall[:, npad + LANE:npad + 2 * LANE]
    qrot = (jnp.concatenate([r1 * cos - r2 * sin, r1 * sin + r2 * cos], axis=-1) * mla_scale).astype(BF16)
    q_ref[...] = (qall[:, :npad] * mla_scale + _dot(qrot, pq_ref[...])).astype(BF16)

    hkv = _rms(proj[:, C_CKV:C_KRA], kvn_ref[...], RMS_EPS).astype(BF16)
    kv = _dot(hkv, wukv_ref[...])
    k1 = proj[:, C_KRA:C_KRB]
    k2 = proj[:, C_KRB:C_DQ]
    krot = jnp.concatenate([k1 * cos - k2 * sin, k1 * sin + k2 * cos], axis=-1).astype(BF16)
    k_ref[...] = (kv[:, :npad] + _dot(krot, pk_ref[...])).astype(BF16)
    v_ref[...] = (kv[:, npad:] + vones_ref[...]).astype(BF16)


def _prep_in_weights(w_in, w_uq, w_ukv):
    d = w_in.shape[0]
    cuts = np.cumsum([512, 256, 128, 32, 384, 384])
    conv, cq, ckv, kr, dq, dk, dv = jnp.split(w_in, [int(c) for c in cuts], axis=-1)
    half = MLA_ROPE // 2
    zpad = jnp.zeros((d, LANE - half), w_in.dtype)
    win_p = jnp.concatenate([conv, cq, ckv, kr[:, :half], zpad, kr[:, half:], zpad, dq, dk, dv], axis=-1)
    assert win_p.shape[1] == C_END

    qr = MLA_NOPE + MLA_ROPE
    rq = w_uq.shape[0]
    wq = w_uq.reshape(rq, MLA_HEADS, qr)
    nope = jnp.pad(wq[:, :, :MLA_NOPE], ((0, 0), (0, 0), (0, LANE - MLA_NOPE))).reshape(rq, MLA_HEADS * LANE)
    r1 = jnp.pad(wq[:, :, MLA_NOPE:MLA_NOPE + half].reshape(rq, MLA_HEADS * half), ((0, 0), (0, LANE - MLA_HEADS * half)))
    r2 = jnp.pad(wq[:, :, MLA_NOPE + half:].reshape(rq, MLA_HEADS * half), ((0, 0), (0, LANE - MLA_HEADS * half)))
    wuq_p = jnp.concatenate([nope, r1, r2], axis=-1)

    rkv = w_ukv.shape[0]
    wkv = w_ukv.reshape(rkv, MLA_HEADS, MLA_NOPE + MLA_V)
    kn = jnp.pad(wkv[:, :, :MLA_NOPE], ((0, 0), (0, 0), (0, LANE - MLA_NOPE)))
    vv = wkv[:, :, MLA_NOPE:]
    v_even = jnp.pad(vv, ((0, 0), (0, 0), (0, LANE - MLA_V)))
    v_odd = jnp.pad(vv, ((0, 0), (0, 0), (LANE - MLA_V, 0)))
    odd = (jnp.arange(MLA_HEADS) % 2 == 1)[None, :, None]
    vp = jnp.where(odd, v_odd, v_even)
    wukv_p = jnp.concatenate([kn.reshape(rkv, -1), vp.reshape(rkv, -1)], axis=-1)
    return win_p.astype(BF16), wuq_p.astype(BF16), wukv_p.astype(BF16)


def _static_tables(seq):
    half = MLA_ROPE // 2
    freq = ROPE_THETA ** (-jnp.arange(half, dtype=F32) / half)
    ang = jnp.arange(seq, dtype=F32)[:, None] * freq[None, :]
    reps = MLA_HEADS
    cos = jnp.pad(jnp.tile(jnp.cos(ang), (1, reps)), ((0, 0), (0, LANE - reps * half)))
    sin = jnp.pad(jnp.tile(jnp.sin(ang), (1, reps)), ((0, 0), (0, LANE - reps * half)))

    pq = np.zeros((2 * LANE, MLA_HEADS * LANE), np.float32)
    pk = np.zeros((2 * LANE, MLA_HEADS * LANE), np.float32)
    vones = np.zeros((1, MLA_HEADS * LANE), np.float32)
    for h in range(MLA_HEADS):
        for i in range(half):
            pq[h * half + i, h * LANE + MLA_NOPE + i] = 1.0
            pq[LANE + h * half + i, h * LANE + MLA_NOPE + half + i] = 1.0
            pk[i, h * LANE + MLA_NOPE + i] = 1.0
            pk[LANE + i, h * LANE + MLA_NOPE + half + i] = 1.0
        vones[0, h * LANE + (MLA_V if h % 2 == 0 else 0)] = 1.0
    return cos, sin, jnp.asarray(pq, BF16), jnp.asarray(pk, BF16), jnp.asarray(vones)


def _pre_mixer(x2, g, win_p, b_glu, q_norm, wuq_p, kv_norm, wukv_p, tables, seq, tm):
    n, d = x2.shape
    cos, sin, pq, pk, vones = tables
    conv_ch = b_glu.shape[-1] // 2
    hp = MLA_HEADS * LANE
    dd = DIFF_HEADS * DIFF_V
    nseq = seq // tm
    full = lambda a: pl.BlockSpec(a.shape, lambda i: (0,) * a.ndim)
    tok = lambda w: pl.BlockSpec((tm, w), lambda i: (i, 0))
    pos = pl.BlockSpec((tm, LANE), lambda i: (i % nseq, 0))
    args = (x2, g, win_p, b_glu, q_norm, wuq_p, kv_norm, wukv_p, vones, pq, pk)
    kern = functools.partial(_pre_mixer_kernel, mla_scale=LOG2E * (MLA_NOPE + MLA_ROPE) ** -0.5,
                             diff_scale=LOG2E * DIFF_HD ** -0.5, conv_ch=conv_ch)
    return pl.pallas_call(
        kern,
        grid=(n // tm,),
        in_specs=[tok(d)] + [full(a) for a in args[1:]] + [pos, pos],
        out_specs=[tok(conv_ch), tok(hp), tok(hp), tok(hp), tok(dd), tok(dd), tok(dd)],
        out_shape=[jax.ShapeDtypeStruct((n, conv_ch), F32)] + [jax.ShapeDtypeStruct((n, hp), BF16)] * 3
        + [jax.ShapeDtypeStruct((n, dd), BF16)] * 3,
        compiler_params=_params("parallel"),
        name="pre_mixer",
    )(*args, cos, sin)


CONV_PAD = 32


def _conv_kernel(u_ref, wdw_ref, bdw_ref, lng_ref, lnb_ref, wpw_ref, bpw_ref, o_ref, upad_ref, *, width, rows):
    seq, ch = u_ref.shape[1], u_ref.shape[2]
    upad_ref[0:CONV_PAD, :] = jnp.zeros((CONV_PAD, ch), F32)
    upad_ref[CONV_PAD:, :] = u_ref[0]
    first = CONV_PAD - (width - 1)
    for c in range(seq // rows):
        acc = jnp.broadcast_to(bdw_ref[...], (rows, ch))
        for j in range(width):
            acc = acc + wdw_ref[j:j + 1, :] * upad_ref[pl.ds(c * rows + first + j, rows), :]
        mu = jnp.mean(acc, axis=-1, keepdims=True)
        cen = acc - mu
        var = jnp.mean(cen * cen, axis=-1, keepdims=True)
        y = cen * lax.rsqrt(var + LN_EPS) * lng_ref[...] + lnb_ref[...]
        y = y * jax.nn.sigmoid(y)
        o_ref[0, c * rows:(c + 1) * rows, :] = (_dot(y.astype(BF16), wpw_ref[...]) + bpw_ref[...]).astype(BF16)


def _conv_module(u3, w_dw, b_dw, ln_g, ln_b, w_pw, b_pw, rows):
    b, seq, ch = u3.shape
    width = w_dw.shape[0]
    assert width - 1 <= CONV_PAD
    full = lambda a: pl.BlockSpec(a.shape, lambda i: (0,) * a.ndim)
    args = (w_dw, b_dw, ln_g, ln_b, w_pw, b_pw)
    return pl.pallas_call(
        functools.partial(_conv_kernel, width=width, rows=rows),
        grid=(b,),
        in_specs=[pl.BlockSpec((1, seq, ch), lambda i: (i, 0, 0))] + [full(a) for a in args],
        out_specs=pl.BlockSpec((1, seq, ch), lambda i: (i, 0, 0)),
        out_shape=jax.ShapeDtypeStruct((b, seq, ch), BF16),
        scratch_shapes=[pltpu.VMEM((seq + CONV_PAD, ch), F32)],
        compiler_params=_params("parallel"),
        name="conv_module",
    )(u3, *args)


def _softmax_step(chunks, v, idx, m_ref, acc_ref, l_ref):
    m_old = m_ref[idx]
    m_cur = functools.reduce(jnp.maximum, chunks)
    m_new = jnp.maximum(m_old, jnp.max(m_cur, axis=-1, keepdims=True))
    p = [jnp.exp2(c - m_new) for c in chunks]
    alpha = jnp.exp2(m_old - m_new)
    if l_ref is not None:
        l_ref[idx] = alpha * l_ref[idx] + functools.reduce(jnp.add, p)
    pb = jnp.concatenate([c.astype(BF16) for c in p], axis=-1)
    acc_ref[idx] = alpha * acc_ref[idx] + _dot(pb, v)
    m_ref[idx] = m_new


def _mla_kernel(q_ref, k_ref, v_ref, o_ref, acc_ref, m_ref, *, tq):
    nb = q_ref.shape[0]
    i = pl.program_id(1)
    nchunk = tq // LANE
    row = lax.broadcasted_iota(jnp.int32, (tq, LANE), 0)
    col = lax.broadcasted_iota(jnp.int32, (tq, LANE), 1)
    m_ref[...] = jnp.full(m_ref.shape, NEG_BIG, F32)
    acc_ref[...] = jnp.zeros(acc_ref.shape, F32)

    def step(j, masked):
        start = pl.multiple_of(j * tq, tq)
        for bb in range(nb):
            for h in range(MLA_HEADS):
                lanes = slice(h * LANE, (h + 1) * LANE)
                s = _dot_nt(q_ref[bb, :, lanes], k_ref[bb, pl.ds(start, tq), lanes])
                chunks = [s[:, c * LANE:(c + 1) * LANE] for c in range(nchunk)]
                if masked:
                    chunks = [jnp.where(row >= col + c * LANE, ch, NEG_BIG) for c, ch in enumerate(chunks)]
                _softmax_step(chunks, v_ref[bb, pl.ds(start, tq), lanes], bb * MLA_HEADS + h, m_ref, acc_ref, None)

    def body(j, carry):
        step(j, False)
        return carry

    lax.fori_loop(0, i, body, 0)
    step(i, True)
    for bb in range(nb):
        for hp in range(HEAD_PAIRS):
            a0 = acc_ref[bb * MLA_HEADS + 2 * hp]
            a1 = acc_ref[bb * MLA_HEADS + 2 * hp + 1]
            o0 = a0 / a0[:, MLA_V:MLA_V + 1]
            o1 = a1 / a1[:, 0:1]
            o_ref[bb, :, hp * LANE:(hp + 1) * LANE] = jnp.where(col < MLA_V, o0, o1).astype(BF16)


def _mla_attention(q3, k3, v3, tq, nb):
    b, seq, w = q3.shape
    stat = pltpu.VMEM((nb * MLA_HEADS, tq, LANE), F32)
    return pl.pallas_call(
        functools.partial(_mla_kernel, tq=tq),
        grid=(b // nb, seq // tq),
        in_specs=[pl.BlockSpec((nb, tq, w), lambda bi, i: (bi, i, 0)),
                  pl.BlockSpec((nb, seq, w), lambda bi, i: (bi, 0, 0)),
                  pl.BlockSpec((nb, seq, w), lambda bi, i: (bi, 0, 0))],
        out_specs=pl.BlockSpec((nb, tq, HEAD_PAIRS * LANE), lambda bi, i: (bi, i, 0)),
        out_shape=jax.ShapeDtypeStruct((b, seq, HEAD_PAIRS * LANE), BF16),
        scratch_shapes=[stat, stat],
        compiler_params=_params("parallel", "arbitrary"),
        name="mla_attention",
    )(q3, k3, v3)


N_DIFF_CHAINS = 2 * DIFF_HEADS


def _diff_kernel(lq1_ref, lk1_ref, lq2_ref, lk2_ref, subln_ref, slope_ref, q_ref, k_ref, v_ref, o_ref,
                 qm_ref, acc_ref, m_ref, l_ref, *, tq, lam_init):
    i = pl.program_id(1)
    nchunk = tq // LANE
    row = lax.broadcasted_iota(jnp.int32, (tq, LANE), 0)
    col = lax.broadcasted_iota(jnp.int32, (tq, LANE), 1)
    colf = lax.broadcasted_iota(jnp.int32, (1, LANE), 1).astype(F32)
    m_ref[...] = jnp.full(m_ref.shape, NEG_BIG, F32)
    l_ref[...] = jnp.zeros(l_ref.shape, F32)
    acc_ref[...] = jnp.zeros(acc_ref.shape, F32)
    for hp in range(HEAD_PAIRS):
        qpair = q_ref[0, :, hp * LANE:(hp + 1) * LANE]
        for c in range(4):
            keep = (col >= c * DIFF_HD) & (col < (c + 1) * DIFF_HD)
            qm_ref[4 * hp + c] = jnp.where(keep, qpair, jnp.zeros_like(qpair))

    def step(j, masked):
        start = pl.multiple_of(j * tq, tq)
        shift = (jnp.zeros((1, LANE), jnp.int32) + (j - i) * tq).astype(F32)
        for hp in range(HEAD_PAIRS):
            lanes = slice(hp * LANE, (hp + 1) * LANE)
            k = k_ref[0, pl.ds(start, tq), lanes]
            v = v_ref[0, pl.ds(start, tq), lanes]
            for c in range(4):
                h = 2 * hp + c // 2
                slope = slope_ref[h:h + 1, :]
                s = _dot_nt(qm_ref[4 * hp + c], k)
                chunks = [s[:, cc * LANE:(cc + 1) * LANE] + slope * (colf + (shift + float(cc * LANE)))
                          for cc in range(nchunk)]
                if masked:
                    chunks = [jnp.where(row >= col + cc * LANE, ch, NEG_BIG) for cc, ch in enumerate(chunks)]
                _softmax_step(chunks, v, 4 * hp + c, m_ref, acc_ref, l_ref)

    def body(j, carry):
        step(j, False)
        return carry

    lax.fori_loop(0, i, body, 0)
    step(i, True)

    lam = (jnp.exp(jnp.sum(lq1_ref[...] * lk1_ref[...], axis=-1, keepdims=True))
           - jnp.exp(jnp.sum(lq2_ref[...] * lk2_ref[...], axis=-1, keepdims=True)) + lam_init)
    left = col < DIFF_V
    for hp in range(HEAD_PAIRS):
        outs = []
        for sub in range(2):
            a1 = 4 * hp + 2 * sub
            a2 = a1 + 1
            l1 = jnp.sum(l_ref[a1], axis=-1, keepdims=True)
            l2 = jnp.sum(l_ref[a2], axis=-1, keepdims=True)
            outs.append(acc_ref[a1] / l1 - lam * (acc_ref[a2] / l2))
        o = jnp.where(left, outs[0], outs[1])
        sq = o * o
        ss_l = jnp.sum(jnp.where(left, sq, 0.0), axis=-1, keepdims=True)
        ss_r = jnp.sum(jnp.where(left, 0.0, sq), axis=-1, keepdims=True)
        ms = jnp.where(left, ss_l, ss_r) * (1.0 / DIFF_V)
        o_ref[0, :, hp * LANE:(hp + 1) * LANE] = (
            o * lax.rsqrt(ms + LN_EPS) * subln_ref[...] * (1.0 - lam_init)).astype(BF16)


def _diff_attention(lq1, lk1, lq2, lk2, subln2, slope_tab, q3, k3, v3, tq, lam_init):
    b, seq, w = q3.shape
    small = lambda a: pl.BlockSpec(a.shape, lambda bi, i: (0,) * a.ndim)
    stat = pltpu.VMEM((N_DIFF_CHAINS, tq, LANE), F32)
    return pl.pallas_call(
        functools.partial(_diff_kernel, tq=tq, lam_init=lam_init),
        grid=(b, seq // tq),
        in_specs=[small(lq1), small(lk1), small(lq2), small(lk2), small(subln2), small(slope_tab),
                  pl.BlockSpec((1, tq, w), lambda bi, i: (bi, i, 0)),
                  pl.BlockSpec((1, seq, w), lambda bi, i: (bi, 0, 0)),
                  pl.BlockSpec((1, seq, w), lambda bi, i: (bi, 0, 0))],
        out_specs=pl.BlockSpec((1, tq, w), lambda bi, i: (bi, i, 0)),
        out_shape=jax.ShapeDtypeStruct((b, seq, w), BF16),
        scratch_shapes=[pltpu.VMEM((N_DIFF_CHAINS, tq, LANE), BF16), stat, stat, stat],
        compiler_params=_params("parallel", "arbitrary"),
        name="diff_attention",
    )(lq1, lk1, lq2, lk2, subln2, slope_tab, q3, k3, v3)


def _store_rows(ref, val):
    rows, d = val.shape
    nc = d // LANE
    for c in range(nc):
        ref[pl.ds(c, rows, stride=nc), :] = val[:, c * LANE:(c + 1) * LANE]


def _load_rows(ref, rows, nc, lead=()):
    return jnp.concatenate([ref[lead + (pl.ds(c, rows, stride=nc), slice(None))] for c in range(nc)], axis=-1)


R_IDX = 0
R_RANK = 2
R_W = 4


def _post_mixer_kernel(x_ref, yc_ref, ym_ref, yd_ref, wout_ref, g_ref, *rest, route):
    if route:
        wr_ref, x1_ref, h_ref, info_ref, cnt_ref, cat_ref, base_ref = rest
    else:
        x1_ref, h_ref, cat_ref = rest
    c0 = yc_ref.shape[1]
    c1 = c0 + ym_ref.shape[1]
    cat_ref[:, 0:c0] = yc_ref[...]
    cat_ref[:, c0:c1] = ym_ref[...]
    cat_ref[:, c1:] = yd_ref[...]
    x1 = x_ref[...] + _dot(cat_ref[...], wout_ref[...])
    x1_ref[...] = x1
    h = _rms(x1, g_ref[...], RMS_EPS)
    if route:
        _store_rows(h_ref, h)
    else:
        h_ref[...] = h.astype(BF16)
    if route:
        n_exp = wr_ref.shape[0]
        tm = h.shape[0]

        @pl.when(pl.program_id(0) == 0)
        def _():
            base_ref[...] = jnp.zeros_like(base_ref)

        lane = lax.broadcasted_iota(jnp.int32, (tm, LANE), 1)
        logits = jnp.full((tm, LANE), NEG_BIG, F32)
        for e in range(n_exp):
            le = jnp.sum(h * wr_ref[e:e + 1, :], axis=-1, keepdims=True)
            logits = jnp.where(lane == e, le, logits)
        m1 = jnp.max(logits, axis=-1, keepdims=True)
        i1 = jnp.min(jnp.where(logits == m1, lane, LANE), axis=-1, keepdims=True)
        rest_l = jnp.where(lane == i1, NEG_BIG, logits)
        m2 = jnp.max(rest_l, axis=-1, keepdims=True)
        i2 = jnp.min(jnp.where(rest_l == m2, lane, LANE), axis=-1, keepdims=True)
        e2 = jnp.exp(m2 - m1)
        w1 = 1.0 / (1.0 + e2)
        w2 = e2 / (1.0 + e2)
        sel = ((lane == i1) | (lane == i2)).astype(F32)
        r = lax.broadcasted_iota(jnp.int32, (tm, tm), 0)
        c = lax.broadcasted_iota(jnp.int32, (tm, tm), 1)
        lower = jnp.where(r > c, 1.0, 0.0).astype(BF16)
        before = _dot(lower, sel.astype(BF16)) + base_ref[...]
        rank1 = jnp.sum(jnp.where(lane == i1, before, 0.0), axis=-1, keepdims=True)
        rank2 = jnp.sum(jnp.where(lane == i2, before, 0.0), axis=-1, keepdims=True)
        total = base_ref[...] + jnp.sum(sel, axis=0, keepdims=True)
        base_ref[...] = total
        cnt_ref[...] = total
        info = jnp.zeros((tm, LANE), F32)
        for ln, val in ((R_IDX, i1.astype(F32)), (R_IDX + 1, i2.astype(F32)), (R_RANK, rank1), (R_RANK + 1, rank2),
                        (R_W, w1), (R_W + 1, w2)):
            info = jnp.where(lane == ln, val, info)
        info_ref[...] = info


def _post_mixer(x2, yc, ym, yd, wout, g, w_router_t, tm):
    n, d = x2.shape
    route = w_router_t is not None
    tok = lambda w: pl.BlockSpec((tm, w), lambda i: (i, 0))
    full = lambda a: pl.BlockSpec(a.shape, lambda i: (0,) * a.ndim)
    dmix = wout.shape[0]
    in_specs = [tok(d), tok(yc.shape[1]), tok(ym.shape[1]), tok(yd.shape[1]), full(wout), full(g)]
    nc = d // LANE
    out_specs = [tok(d), pl.BlockSpec((tm * nc, LANE), lambda i: (i, 0)) if route else tok(d)]
    out_shape = [jax.ShapeDtypeStruct((n, d), F32),
                 jax.ShapeDtypeStruct((n * nc, LANE), F32) if route else jax.ShapeDtypeStruct((n, d), BF16)]
    scratch = [pltpu.VMEM((tm, dmix), BF16)]
    args = [x2, yc, ym, yd, wout, g]
    if route:
        in_specs.append(full(w_router_t))
        out_specs += [tok(LANE), pl.BlockSpec((1, LANE), lambda i: (0, 0))]
        out_shape += [jax.ShapeDtypeStruct((n, LANE), F32), jax.ShapeDtypeStruct((1, LANE), F32)]
        scratch.append(pltpu.VMEM((1, LANE), F32))
        args.append(w_router_t)
    return pl.pallas_call(
        functools.partial(_post_mixer_kernel, route=route),
        grid=(n // tm,),
        in_specs=in_specs,
        out_specs=out_specs,
        out_shape=out_shape,
        scratch_shapes=scratch,
        compiler_params=_params("arbitrary" if route else "parallel"),
        name="post_mixer_route" if route else "post_mixer",
    )(*args)


def _swiglu_part(h, wg_ref, wu_ref, wd_ref):
    a = _dot(h, wg_ref[0])
    b = _dot(h, wu_ref[0])
    return _dot((a * jax.nn.sigmoid(a) * b).astype(BF16), wd_ref[0])


def _ffn_dense_kernel(h_ref, wg_ref, wu_ref, wd_ref, x1_ref, g_ref, o_ref, *, final):
    f = pl.program_id(1)
    part = _swiglu_part(h_ref[...], wg_ref, wu_ref, wd_ref)

    @pl.when(f == 0)
    def _():
        o_ref[...] = x1_ref[...] + part

    @pl.when(f > 0)
    def _():
        o_ref[...] += part

    if final:
        @pl.when(f == pl.num_programs(1) - 1)
        def _():
            o_ref[...] = _rms(o_ref[...], g_ref[...], RMS_EPS)


def _ffn_dense(h, wg, wu, wd, x1, g, tm, tf, final):
    n, d = x1.shape
    dff = wg.shape[-1]
    return pl.pallas_call(
        functools.partial(_ffn_dense_kernel, final=final),
        grid=(n // tm, dff // tf),
        in_specs=[pl.BlockSpec((tm, d), lambda i, f: (i, 0)),
                  pl.BlockSpec((1, d, tf), lambda i, f: (0, 0, f)),
                  pl.BlockSpec((1, d, tf), lambda i, f: (0, 0, f)),
                  pl.BlockSpec((1, tf, d), lambda i, f: (0, f, 0)),
                  pl.BlockSpec((tm, d), lambda i, f: (i, 0)),
                  pl.BlockSpec((1, d), lambda i, f: (0, 0))],
        out_specs=pl.BlockSpec((tm, d), lambda i, f: (i, 0)),
        out_shape=jax.ShapeDtypeStruct((n, d), F32),
        compiler_params=_params("parallel", "arbitrary"),
        name="ffn_dense",
    )(h, wg, wu, wd, x1, g)


def _row_copies(src_of, dst_of, n_rows, sem):
    def start(r, carry):
        for k in range(TOP_K):
            pltpu.make_async_copy(src_of(r, k), dst_of(r, k), sem).start()
        return carry

    def wait(r, carry):
        for k in range(TOP_K):
            pltpu.make_async_copy(src_of(r, k), dst_of(r, k), sem).wait()
        return carry

    lax.fori_loop(0, n_rows, start, 0, unroll=8)
    lax.fori_loop(0, n_rows, wait, 0, unroll=8)


def _row_at(ref, r, nc, n=1, lead=()):
    return ref.at[lead + (pl.ds(pl.multiple_of(r * nc, nc), n * nc),)]


def _scatter_kernel(zstart_ref, pos_ref, h_ref, hs_ref, zeros_ref, sem, *, tmr, nc):
    tm = h_ref.shape[0] // nc

    @pl.when(pl.program_id(0) == 0)
    def _():
        zeros_ref[...] = jnp.zeros_like(zeros_ref)
        for e in range(zstart_ref.shape[0]):
            fill = pltpu.make_async_copy(zeros_ref, _row_at(hs_ref, zstart_ref[e], nc, tmr), sem)
            fill.start()
            fill.wait()

    _row_copies(lambda r, k: _row_at(h_ref, r, nc),
                lambda r, k: _row_at(hs_ref, pos_ref[0, 0, TOP_K * r + k], nc), tm, sem)


def _moe_scatter(zstart, pos3, h, rows, tm, tmr, nc):
    n = h.shape[0] // nc
    grid_spec = pltpu.PrefetchScalarGridSpec(
        num_scalar_prefetch=1,
        grid=(n // tm,),
        in_specs=[pl.BlockSpec((1, 1, TOP_K * tm), lambda i, zs: (i, 0, 0), memory_space=pltpu.SMEM),
                  pl.BlockSpec((tm * nc, LANE), lambda i, zs: (i, 0))],
        out_specs=pl.BlockSpec(memory_space=pl.ANY),
        scratch_shapes=[pltpu.VMEM((tmr * nc, LANE), F32), pltpu.SemaphoreType.DMA(())],
    )
    return pl.pallas_call(
        functools.partial(_scatter_kernel, tmr=tmr, nc=nc),
        grid_spec=grid_spec,
        out_shape=jax.ShapeDtypeStruct((rows * nc, LANE), F32),
        compiler_params=_params("arbitrary"),
        name="moe_scatter",
    )(zstart, pos3, h)


def _ffn_grouped_kernel(te_ref, nv_ref, hs_ref, wg_ref, wu_ref, wd_ref, ys_ref, acc_ref):
    f = pl.program_id(1)
    tmr, d = acc_ref.shape

    @pl.when(pl.program_id(0) < nv_ref[0])
    def _():
        h = _load_rows(hs_ref, tmr, d // LANE).astype(BF16)
        part = _swiglu_part(h, wg_ref, wu_ref, wd_ref)

        @pl.when(f == 0)
        def _():
            acc_ref[...] = part

        @pl.when(f > 0)
        def _():
            acc_ref[...] += part

        @pl.when(f == pl.num_programs(1) - 1)
        def _():
            _store_rows(ys_ref, acc_ref[...])

    @pl.when((pl.program_id(0) >= nv_ref[0]) & (f == pl.num_programs(1) - 1))
    def _():
        ys_ref[...] = jnp.zeros_like(ys_ref)


def _ffn_grouped(tile_expert, n_valid, hs, wg, wu, wd, n_tiles, tmr, tf):
    d = wg.shape[1]
    dff = wg.shape[-1]
    nc = d // LANE
    row_blk = lambda j, f, te, nv: (jnp.minimum(j, nv[0] - 1), 0)
    grid_spec = pltpu.PrefetchScalarGridSpec(
        num_scalar_prefetch=2,
        grid=(n_tiles, dff // tf),
        in_specs=[pl.BlockSpec((tmr * nc, LANE), row_blk),
                  pl.BlockSpec((1, d, tf), lambda j, f, te, nv: (te[j], 0, f)),
                  pl.BlockSpec((1, d, tf), lambda j, f, te, nv: (te[j], 0, f)),
                  pl.BlockSpec((1, tf, d), lambda j, f, te, nv: (te[j], f, 0))],
        out_specs=pl.BlockSpec((tmr * nc, LANE), lambda j, f, te, nv: (j, 0)),
        scratch_shapes=[pltpu.VMEM((tmr, d), F32)],
    )
    return pl.pallas_call(
        _ffn_grouped_kernel,
        grid_spec=grid_spec,
        out_shape=jax.ShapeDtypeStruct((n_tiles * tmr * nc, LANE), F32),
        compiler_params=_params("arbitrary", "arbitrary"),
        name="ffn_grouped",
    )(tile_expert, n_valid, hs, wg, wu, wd)


def _combine_kernel(pos_ref, x1_ref, info_ref, g_ref, ys_ref, o_ref, ybuf_ref, sem, *, final):
    tm, d = x1_ref.shape
    nc = d // LANE
    _row_copies(lambda r, k: _row_at(ys_ref, pos_ref[0, 0, TOP_K * r + k], nc),
                lambda r, k: _row_at(ybuf_ref, r, nc, lead=(k,)), tm, sem)
    info = info_ref[...]
    y = (x1_ref[...] + info[:, R_W:R_W + 1] * _load_rows(ybuf_ref, tm, nc, (0,))
         + info[:, R_W + 1:R_W + 2] * _load_rows(ybuf_ref, tm, nc, (1,)))
    o_ref[...] = _rms(y, g_ref[...], RMS_EPS) if final else y


def _moe_combine(pos3, x1, info, g, ys, tm, final):
    n, d = x1.shape
    nc = d // LANE
    return pl.pallas_call(
        functools.partial(_combine_kernel, final=final),
        grid=(n // tm,),
        in_specs=[pl.BlockSpec((1, 1, TOP_K * tm), lambda i: (i, 0, 0), memory_space=pltpu.SMEM),
                  pl.BlockSpec((tm, d), lambda i: (i, 0)),
                  pl.BlockSpec((tm, LANE), lambda i: (i, 0)),
                  pl.BlockSpec((1, d), lambda i: (0, 0)),
                  pl.BlockSpec(memory_space=pl.ANY)],
        out_specs=pl.BlockSpec((tm, d), lambda i: (i, 0)),
        out_shape=jax.ShapeDtypeStruct((n, d), F32),
        scratch_shapes=[pltpu.VMEM((TOP_K, tm * nc, LANE), F32), pltpu.SemaphoreType.DMA(())],
        compiler_params=_params("arbitrary"),
        name="moe_combine",
    )(pos3, x1, info, g, ys)


def _moe_plan(info, counts, n_exp, tm, tmr):
    n = info.shape[0]
    idx = info[:, R_IDX:R_IDX + TOP_K].astype(jnp.int32)
    rank = info[:, R_RANK:R_RANK + TOP_K].astype(jnp.int32)
    cnt = counts[0, :n_exp].astype(jnp.int32)
    padded = (cnt + tmr - 1) // tmr * tmr
    ends = jnp.cumsum(padded)
    offs = ends - padded
    pos = rank + jnp.sum(jnp.where(idx[..., None] == jnp.arange(n_exp), offs, 0), axis=-1)
    n_tiles = TOP_K * n // tmr + n_exp
    n_valid = ends[-1] // tmr
    tile_row = jnp.minimum(jnp.arange(n_tiles), n_valid - 1) * tmr
    tile_expert = jnp.sum(tile_row[:, None] >= ends[None, :], axis=-1).astype(jnp.int32)
    pos3 = pos.reshape(n // tm, 1, TOP_K * tm)
    tail = jnp.minimum(ends[-1] + jnp.arange(n_exp + 1) * tmr, n_tiles * tmr)
    zstart = jnp.concatenate([offs + cnt, tail]).astype(jnp.int32)
    return pos3, zstart, tile_expert, n_valid.reshape(1).astype(jnp.int32), n_tiles


def _pick(n, pref):
    t = min(n, pref)
    while n % t:
        t //= 2
    return t


def kernel(x, attn_norm, w_in, b_glu, w_dw, b_dw, conv_ln_g, conv_ln_b, w_pw, b_pw, q_norm, w_uq, kv_norm, w_ukv, lambda_q1, lambda_k1, lambda_q2, lambda_k2, diff_subln, w_out, ffn_norm, w_gate_dense, w_up_dense, w_down_dense, w_router, w_gate_exp, w_up_exp, w_down_exp, final_norm):
    bsz, seq, d = x.shape
    n = bsz * seq
    depth = attn_norm.shape[0]
    dff = w_gate_dense.shape[-1]
    tm = _pick(seq, 512)
    tq = _pick(seq, 256)
    tm_ffn = _pick(n, 512)
    tf = dff // 2 if dff % (2 * LANE) == 0 else dff
    conv_rows = _pick(seq, 256)
    row = lambda a: a.reshape(1, -1)

    tables = _static_tables(seq)
    slope_tab = jnp.broadcast_to(
        (LOG2E * 2.0 ** (-8.0 * jnp.arange(1, DIFF_HEADS + 1, dtype=F32) / DIFF_HEADS))[:, None], (DIFF_HEADS, LANE))

    x2 = x.reshape(n, d)
    for l in range(depth):
        win_p, wuq_p, wukv_p = _prep_in_weights(w_in[l], w_uq[l], w_ukv[l])
        u, q, k, v, dq, dk, dv = _pre_mixer(x2, row(attn_norm[l]), win_p, row(b_glu[l]), row(q_norm[l]), wuq_p,
                                            row(kv_norm[l]), wukv_p, tables, seq, tm)
        r3 = lambda a: a.reshape(bsz, seq, a.shape[-1])
        yc = _conv_module(r3(u), w_dw[l], row(b_dw[l]), row(conv_ln_g[l]), row(conv_ln_b[l]),
                          w_pw[l].astype(BF16), row(b_pw[l]), conv_rows)
        ym = _mla_attention(r3(q), r3(k), r3(v), tq, 2 if bsz % 2 == 0 else 1)
        lam_init = 0.8 - 0.6 * math.exp(-0.3 * l)
        subln2 = jnp.tile(row(diff_subln[l]), (1, LANE // DIFF_V))
        yd = _diff_attention(row(lambda_q1[l]), row(lambda_k1[l]), row(lambda_q2[l]), row(lambda_k2[l]), subln2,
                             slope_tab, r3(dq), r3(dk), r3(dv), tq, lam_init)
        flat = lambda a: a.reshape(n, a.shape[-1])
        final = l == depth - 1
        fin_g = row(final_norm) if final else row(ffn_norm[l])
        j = l // 2
        if l % 2 == 0:
            x1, h2 = _post_mixer(x2, flat(yc), flat(ym), flat(yd), w_out[l].astype(BF16), row(ffn_norm[l]), None, tm)
            x2 = _ffn_dense(h2, w_gate_dense[j][None].astype(BF16), w_up_dense[j][None].astype(BF16),
                            w_down_dense[j][None].astype(BF16), x1, fin_g, tm_ffn, tf, final)
        else:
            x1, h2, info, counts = _post_mixer(x2, flat(yc), flat(ym), flat(yd), w_out[l].astype(BF16),
                                               row(ffn_norm[l]), w_router[j].T, tm)
            n_exp = w_router.shape[-1]
            pos3, zstart, tile_expert, n_valid, n_tiles = _moe_plan(info, counts, n_exp, tm, tm_ffn)
            nc = d // LANE
            assert nc % 8 == 0, "row DMAs need rows that are whole (8, 128) tiles"
            hs = _moe_scatter(zstart, pos3, h2, (n_tiles + 1) * tm_ffn, tm, tm_ffn, nc)
            ys = _ffn_grouped(tile_expert, n_valid, hs, w_gate_exp[j].astype(BF16), w_up_exp[j].astype(BF16),
                              w_down_exp[j].astype(BF16), n_tiles, tm_ffn, tf)
            x2 = _moe_combine(pos3, x1, info, fin_g, ys, tm, final)
    return x2.reshape(bsz, seq, d)
```

```python
import functools
import math

import numpy as np
import jax
import jax.numpy as jnp
from jax import lax
from jax.experimental import pallas as pl
from jax.experimental.pallas import tpu as pltpu

MLA_HEADS = 6
MLA_NOPE = 64
MLA_ROPE = 32
MLA_V = 64
DIFF_HEADS = 6
DIFF_HD = 32
DIFF_V = 2 * DIFF_HD
TOP_K = 2
ROPE_THETA = 10000.0
RMS_EPS = 1e-6
LN_EPS = 1e-5

LANE = 128
NEG_BIG = -1e30
LOG2E = math.log2(math.e)
VMEM_LIMIT = 56 * 1024 * 1024

F32 = jnp.float32
BF16 = jnp.bfloat16

HEAD_PAIRS = MLA_HEADS // 2
assert MLA_HEADS == DIFF_HEADS and MLA_HEADS % 2 == 0
assert MLA_V == DIFF_V == LANE // 2 and MLA_NOPE + MLA_ROPE <= LANE


def _rms(x, g, eps):
    return x * lax.rsqrt(jnp.mean(x * x, axis=-1, keepdims=True) + eps) * g


def _dot(a, b):
    return jnp.dot(a, b, preferred_element_type=F32)


def _dot_nt(a, b):
    return lax.dot_general(a, b, (((1,), (1,)), ((), ())), preferred_element_type=F32)


def _params(*sem):
    return pltpu.CompilerParams(dimension_semantics=sem, vmem_limit_bytes=VMEM_LIMIT)


C_CONV = 0
C_CQ = 512
C_CKV = 768
C_KRA = 896
C_KRB = 1024
C_DQ = 1152
C_DK = 1536
C_DV = 1920
C_END = 2304


def _pre_mixer_kernel(x_ref, g_ref, win_ref, bglu_ref, qn_ref, wuq_ref, kvn_ref, wukv_ref, vones_ref,
                      pq_ref, pk_ref, cos_ref, sin_ref,
                      u_ref, q_ref, k_ref, v_ref, dq_ref, dk_ref, dv_ref, *, mla_scale, diff_scale, conv_ch):
    h = _rms(x_ref[...], g_ref[...], RMS_EPS).astype(BF16)
    proj = _dot(h, win_ref[...])

    a = proj[:, C_CONV:C_CONV + 2 * conv_ch] + bglu_ref[...]
    u_ref[...] = a[:, :conv_ch] * jax.nn.sigmoid(a[:, conv_ch:])

    dq_ref[...] = (proj[:, C_DQ:C_DK] * diff_scale).astype(BF16)
    dk_ref[...] = proj[:, C_DK:C_DV].astype(BF16)
    dv_ref[...] = proj[:, C_DV:C_END].astype(BF16)

    cos = cos_ref[...]
    sin = sin_ref[...]

    hq = _rms(proj[:, C_CQ:C_CKV], qn_ref[...], RMS_EPS).astype(BF16)
    qall = _dot(hq, wuq_ref[...])
    npad = MLA_HEADS * LANE
    r1 = qall[:, npad:npad + LANE]
    r2 = qall[:, npad + LANE:npad + 2 * LANE]
    qrot = (jnp.concatenate([r1 * cos - r2 * sin, r1 * sin + r2 * cos], axis=-1) * mla_scale).astype(BF16)
    q_ref[...] = (qall[:, :npad] * mla_scale + _dot(qrot, pq_ref[...])).astype(BF16)

    hkv = _rms(proj[:, C_CKV:C_KRA], kvn_ref[...], RMS_EPS).astype(BF16)
    kv = _dot(hkv, wukv_ref[...])
    k1 = proj[:, C_KRA:C_KRB]
    k2 = proj[:, C_KRB:C_DQ]
    krot = jnp.concatenate([k1 * cos - k2 * sin, k1 * sin + k2 * cos], axis=-1).astype(BF16)
    k_ref[...] = (kv[:, :npad] + _dot(krot, pk_ref[...])).astype(BF16)
    v_ref[...] = (kv[:, npad:] + vones_ref[...]).astype(BF16)


def _prep_in_weights(w_in, w_uq, w_ukv):
    d = w_in.shape[0]
    cuts = np.cumsum([512, 256, 128, 32, 384, 384])
    conv, cq, ckv, kr, dq, dk, dv = jnp.split(w_in, [int(c) for c in cuts], axis=-1)
    half = MLA_ROPE // 2
    zpad = jnp.zeros((d, LANE - half), w_in.dtype)
    win_p = jnp.concatenate([conv, cq, ckv, kr[:, :half], zpad, kr[:, half:], zpad, dq, dk, dv], axis=-1)
    assert win_p.shape[1] == C_END

    qr = MLA_NOPE + MLA_ROPE
    rq = w_uq.shape[0]
    wq = w_uq.reshape(rq, MLA_HEADS, qr)
    nope = jnp.pad(wq[:, :, :MLA_NOPE], ((0, 0), (0, 0), (0, LANE - MLA_NOPE))).reshape(rq, MLA_HEADS * LANE)
    r1 = jnp.pad(wq[:, :, MLA_NOPE:MLA_NOPE + half].reshape(rq, MLA_HEADS * half), ((0, 0), (0, LANE - MLA_HEADS * half)))
    r2 = jnp.pad(wq[:, :, MLA_NOPE + half:].reshape(rq, MLA_HEADS * half), ((0, 0), (0, LANE - MLA_HEADS * half)))
    wuq_p = jnp.concatenate([nope, r1, r2], axis=-1)

    rkv = w_ukv.shape[0]
    wkv = w_ukv.reshape(rkv, MLA_HEADS, MLA_NOPE + MLA_V)
    kn = jnp.pad(wkv[:, :, :MLA_NOPE], ((0, 0), (0, 0), (0, LANE - MLA_NOPE)))
    vv = wkv[:, :, MLA_NOPE:]
    v_even = jnp.pad(vv, ((0, 0), (0, 0), (0, LANE - MLA_V)))
    v_odd = jnp.pad(vv, ((0, 0), (0, 0), (LANE - MLA_V, 0)))
    odd = (jnp.arange(MLA_HEADS) % 2 == 1)[None, :, None]
    vp = jnp.where(odd, v_odd, v_even)
    wukv_p = jnp.concatenate([kn.reshape(rkv, -1), vp.reshape(rkv, -1)], axis=-1)
    return win_p.astype(BF16), wuq_p.astype(BF16), wukv_p.astype(BF16)


def _static_tables(seq):
    half = MLA_ROPE // 2
    freq = ROPE_THETA ** (-jnp.arange(half, dtype=F32) / half)
    ang = jnp.arange(seq, dtype=F32)[:, None] * freq[None, :]
    reps = MLA_HEADS
    cos = jnp.pad(jnp.tile(jnp.cos(ang), (1, reps)), ((0, 0), (0, LANE - reps * half)))
    sin = jnp.pad(jnp.tile(jnp.sin(ang), (1, reps)), ((0, 0), (0, LANE - reps * half)))

    pq = np.zeros((2 * LANE, MLA_HEADS * LANE), np.float32)
    pk = np.zeros((2 * LANE, MLA_HEADS * LANE), np.float32)
    vones = np.zeros((1, MLA_HEADS * LANE), np.float32)
    for h in range(MLA_HEADS):
        for i in range(half):
            pq[h * half + i, h * LANE + MLA_NOPE + i] = 1.0
            pq[LANE + h * half + i, h * LANE + MLA_NOPE + half + i] = 1.0
            pk[i, h * LANE + MLA_NOPE + i] = 1.0
            pk[LANE + i, h * LANE + MLA_NOPE + half + i] = 1.0
        vones[0, h * LANE + (MLA_V if h % 2 == 0 else 0)] = 1.0
    return cos, sin, jnp.asarray(pq, BF16), jnp.asarray(pk, BF16), jnp.asarray(vones)


def _pre_mixer(x2, g, win_p, b_glu, q_norm, wuq_p, kv_norm, wukv_p, tables, seq, tm):
    n, d = x2.shape
    cos, sin, pq, pk, vones = tables
    conv_ch = b_glu.shape[-1] // 2
    hp = MLA_HEADS * LANE
    dd = DIFF_HEADS * DIFF_V
    nseq = seq // tm
    full = lambda a: pl.BlockSpec(a.shape, lambda i: (0,) * a.ndim)
    tok = lambda w: pl.BlockSpec((tm, w), lambda i: (i, 0))
    pos = pl.BlockSpec((tm, LANE), lambda i: (i % nseq, 0))
    args = (x2, g, win_p, b_glu, q_norm, wuq_p, kv_norm, wukv_p, vones, pq, pk)
    kern = functools.partial(_pre_mixer_kernel, mla_scale=LOG2E * (MLA_NOPE + MLA_ROPE) ** -0.5,
                             diff_scale=LOG2E * DIFF_HD ** -0.5, conv_ch=conv_ch)
    return pl.pallas_call(
        kern,
        grid=(n // tm,),
        in_specs=[tok(d)] + [full(a) for a in args[1:]] + [pos, pos],
        out_specs=[tok(conv_ch), tok(hp), tok(hp), tok(hp), tok(dd), tok(dd), tok(dd)],
        out_shape=[jax.ShapeDtypeStruct((n, conv_ch), F32)] + [jax.ShapeDtypeStruct((n, hp), BF16)] * 3
        + [jax.ShapeDtypeStruct((n, dd), BF16)] * 3,
        compiler_params=_params("parallel"),
        name="pre_mixer",
    )(*args, cos, sin)


CONV_PAD = 32


def _conv_kernel(u_ref, wdw_ref, bdw_ref, lng_ref, lnb_ref, wpw_ref, bpw_ref, o_ref, upad_ref, *, width, rows):
    seq, ch = u_ref.shape[1], u_ref.shape[2]
    upad_ref[0:CONV_PAD, :] = jnp.zeros((CONV_PAD, ch), F32)
    upad_ref[CONV_PAD:, :] = u_ref[0]
    first = CONV_PAD - (width - 1)
    for c in range(seq // rows):
        acc = jnp.broadcast_to(bdw_ref[...], (rows, ch))
        for j in range(width):
            acc = acc + wdw_ref[j:j + 1, :] * upad_ref[pl.ds(c * rows + first + j, rows), :]
        mu = jnp.mean(acc, axis=-1, keepdims=True)
        cen = acc - mu
        var = jnp.mean(cen * cen, axis=-1, keepdims=True)
        y = cen * lax.rsqrt(var + LN_EPS) * lng_ref[...] + lnb_ref[...]
        y = y * jax.nn.sigmoid(y)
        o_ref[0, c * rows:(c + 1) * rows, :] = (_dot(y.astype(BF16), wpw_ref[...]) + bpw_ref[...]).astype(BF16)


def _conv_module(u3, w_dw, b_dw, ln_g, ln_b, w_pw, b_pw, rows):
    b, seq, ch = u3.shape
    width = w_dw.shape[0]
    assert width - 1 <= CONV_PAD
    full = lambda a: pl.BlockSpec(a.shape, lambda i: (0,) * a.ndim)
    args = (w_dw, b_dw, ln_g, ln_b, w_pw, b_pw)
    return pl.pallas_call(
        functools.partial(_conv_kernel, width=width, rows=rows),
        grid=(b,),
        in_specs=[pl.BlockSpec((1, seq, ch), lambda i: (i, 0, 0))] + [full(a) for a in args],
        out_specs=pl.BlockSpec((1, seq, ch), lambda i: (i, 0, 0)),
        out_shape=jax.ShapeDtypeStruct((b, seq, ch), BF16),
        scratch_shapes=[pltpu.VMEM((seq + CONV_PAD, ch), F32)],
        compiler_params=_params("parallel"),
        name="conv_module",
    )(u3, *args)


def _softmax_step(chunks, v, idx, m_ref, acc_ref, l_ref):
    m_old = m_ref[idx]
    m_cur = functools.reduce(jnp.maximum, chunks)
    m_new = jnp.maximum(m_old, jnp.max(m_cur, axis=-1, keepdims=True))
    p = [jnp.exp2(c - m_new) for c in chunks]
    alpha = jnp.exp2(m_old - m_new)
    if l_ref is not None:
        l_ref[idx] = alpha * l_ref[idx] + functools.reduce(jnp.add, p)
    pb = jnp.concatenate([c.astype(BF16) for c in p], axis=-1)
    acc_ref[idx] = alpha * acc_ref[idx] + _dot(pb, v)
    m_ref[idx] = m_new


def _mla_kernel(q_ref, k_ref, v_ref, o_ref, acc_ref, m_ref, *, tq):
    nb = q_ref.shape[0]
    i = pl.program_id(1)
    nchunk = tq // LANE
    row = lax.broadcasted_iota(jnp.int32, (tq, LANE), 0)
    col = lax.broadcasted_iota(jnp.int32, (tq, LANE), 1)
    m_ref[...] = jnp.full(m_ref.shape, NEG_BIG, F32)
    acc_ref[...] = jnp.zeros(acc_ref.shape, F32)

    def step(j, masked):
        start = pl.multiple_of(j * tq, tq)
        for bb in range(nb):
            for h in range(MLA_HEADS):
                lanes = slice(h * LANE, (h + 1) * LANE)
                s = _dot_nt(q_ref[bb, :, lanes], k_ref[bb, pl.ds(start, tq), lanes])
                chunks = [s[:, c * LANE:(c + 1) * LANE] for c in range(nchunk)]
                if masked:
                    chunks = [jnp.where(row >= col + c * LANE, ch, NEG_BIG) for c, ch in enumerate(chunks)]
                _softmax_step(chunks, v_ref[bb, pl.ds(start, tq), lanes], bb * MLA_HEADS + h, m_ref, acc_ref, None)

    def body(j, carry):
        step(j, False)
        return carry

    def diag_body(j, carry):
        step(j, True)
        return carry

    lax.fori_loop(0, i, body, 0)
    lax.fori_loop(i, i + 1, diag_body, 0)
    for bb in range(nb):
        for hp in range(HEAD_PAIRS):
            a0 = acc_ref[bb * MLA_HEADS + 2 * hp]
            a1 = acc_ref[bb * MLA_HEADS + 2 * hp + 1]
            o0 = a0 / a0[:, MLA_V:MLA_V + 1]
            o1 = a1 / a1[:, 0:1]
            o_ref[bb, :, hp * LANE:(hp + 1) * LANE] = jnp.where(col < MLA_V, o0, o1).astype(BF16)


def _mla_attention(q3, k3, v3, tq, nb):
    b, seq, w = q3.shape
    stat = pltpu.VMEM((nb * MLA_HEADS, tq, LANE), F32)
    return pl.pallas_call(
        functools.partial(_mla_kernel, tq=tq),
        grid=(b // nb, seq // tq),
        in_specs=[pl.BlockSpec((nb, tq, w), lambda bi, i: (bi, i, 0)),
                  pl.BlockSpec((nb, seq, w), lambda bi, i: (bi, 0, 0)),
                  pl.BlockSpec((nb, seq, w), lambda bi, i: (bi, 0, 0))],
        out_specs=pl.BlockSpec((nb, tq, HEAD_PAIRS * LANE), lambda bi, i: (bi, i, 0)),
        out_shape=jax.ShapeDtypeStruct((b, seq, HEAD_PAIRS * LANE), BF16),
        scratch_shapes=[stat, stat],
        compiler_params=_params("parallel", "arbitrary"),
        name="mla_attention",
    )(q3, k3, v3)


N_DIFF_CHAINS = 2 * DIFF_HEADS


def _diff_kernel(lq1_ref, lk1_ref, lq2_ref, lk2_ref, subln_ref, slope_ref, q_ref, k_ref, v_ref, o_ref,
                 qm_ref, acc_ref, m_ref, l_ref, *, tq, lam_init):
    i = pl.program_id(1)
    nchunk = tq // LANE
    row = lax.broadcasted_iota(jnp.int32, (tq, LANE), 0)
    col = lax.broadcasted_iota(jnp.int32, (tq, LANE), 1)
    colf = lax.broadcasted_iota(jnp.int32, (1, LANE), 1).astype(F32)
    m_ref[...] = jnp.full(m_ref.shape, NEG_BIG, F32)
    l_ref[...] = jnp.zeros(l_ref.shape, F32)
    acc_ref[...] = jnp.zeros(acc_ref.shape, F32)
    for hp in range(HEAD_PAIRS):
        qpair = q_ref[0, :, hp * LANE:(hp + 1) * LANE]
        for c in range(4):
            keep = (col >= c * DIFF_HD) & (col < (c + 1) * DIFF_HD)
            qm_ref[4 * hp + c] = jnp.where(keep, qpair, jnp.zeros_like(qpair))

    def step(j, masked):
        start = pl.multiple_of(j * tq, tq)
        shift = (jnp.zeros((1, LANE), jnp.int32) + (j - i) * tq).astype(F32)
        for hp in range(HEAD_PAIRS):
            lanes = slice(hp * LANE, (hp + 1) * LANE)
            k = k_ref[0, pl.ds(start, tq), lanes]
            v = v_ref[0, pl.ds(start, tq), lanes]
            for c in range(4):
                h = 2 * hp + c // 2
                slope = slope_ref[h:h + 1, :]
                s = _dot_nt(qm_ref[4 * hp + c], k)
                chunks = [s[:, cc * LANE:(cc + 1) * LANE] + slope * (colf + (shift + float(cc * LANE)))
                          for cc in range(nchunk)]
                if masked:
                    chunks = [jnp.where(row >= col + cc * LANE, ch, NEG_BIG) for cc, ch in enumerate(chunks)]
                _softmax_step(chunks, v, 4 * hp + c, m_ref, acc_ref, l_ref)

    def body(j, carry):
        step(j, False)
        return carry

    lax.fori_loop(0, i, body, 0)
    step(i, True)

    lam = (jnp.exp(jnp.sum(lq1_ref[...] * lk1_ref[...], axis=-1, keepdims=True))
           - jnp.exp(jnp.sum(lq2_ref[...] * lk2_ref[...], axis=-1, keepdims=True)) + lam_init)
    left = col < DIFF_V
    for hp in range(HEAD_PAIRS):
        outs = []
        for sub in range(2):
            a1 = 4 * hp + 2 * sub
            a2 = a1 + 1
            l1 = jnp.sum(l_ref[a1], axis=-1, keepdims=True)
            l2 = jnp.sum(l_ref[a2], axis=-1, keepdims=True)
            outs.append(acc_ref[a1] / l1 - lam * (acc_ref[a2] / l2))
        o = jnp.where(left, outs[0], outs[1])
        sq = o * o
        ss_l = jnp.sum(jnp.where(left, sq, 0.0), axis=-1, keepdims=True)
        ss_r = jnp.sum(jnp.where(left, 0.0, sq), axis=-1, keepdims=True)
        ms = jnp.where(left, ss_l, ss_r) * (1.0 / DIFF_V)
        o_ref[0, :, hp * LANE:(hp + 1) * LANE] = (
            o * lax.rsqrt(ms + LN_EPS) * subln_ref[...] * (1.0 - lam_init)).astype(BF16)


def _diff_attention(lq1, lk1, lq2, lk2, subln2, slope_tab, q3, k3, v3, tq, lam_init):
    b, seq, w = q3.shape
    small = lambda a: pl.BlockSpec(a.shape, lambda bi, i: (0,) * a.ndim)
    stat = pltpu.VMEM((N_DIFF_CHAINS, tq, LANE), F32)
    return pl.pallas_call(
        functools.partial(_diff_kernel, tq=tq, lam_init=lam_init),
        grid=(b, seq // tq),
        in_specs=[small(lq1), small(lk1), small(lq2), small(lk2), small(subln2), small(slope_tab),
                  pl.BlockSpec((1, tq, w), lambda bi, i: (bi, i, 0)),
                  pl.BlockSpec((1, seq, w), lambda bi, i: (bi, 0, 0)),
                  pl.BlockSpec((1, seq, w), lambda bi, i: (bi, 0, 0))],
        out_specs=pl.BlockSpec((1, tq, w), lambda bi, i: (bi, i, 0)),
        out_shape=jax.ShapeDtypeStruct((b, seq, w), BF16),
        scratch_shapes=[pltpu.VMEM((N_DIFF_CHAINS, tq, LANE), BF16), stat, stat, stat],
        compiler_params=_params("parallel", "arbitrary"),
        name="diff_attention",
    )(lq1, lk1, lq2, lk2, subln2, slope_tab, q3, k3, v3)


def _store_rows(ref, val):
    rows, d = val.shape
    nc = d // LANE
    for c in range(nc):
        ref[pl.ds(c, rows, stride=nc), :] = val[:, c * LANE:(c + 1) * LANE]


def _load_rows(ref, rows, nc, lead=()):
    return jnp.concatenate([ref[lead + (pl.ds(c, rows, stride=nc), slice(None))] for c in range(nc)], axis=-1)


R_IDX = 0
R_RANK = 2
R_W = 4


def _post_mixer_kernel(x_ref, yc_ref, ym_ref, yd_ref, wout_ref, g_ref, *rest, route):
    if route:
        wr_ref, x1_ref, h_ref, info_ref, cnt_ref, cat_ref, base_ref = rest
    else:
        x1_ref, h_ref, cat_ref = rest
    c0 = yc_ref.shape[1]
    c1 = c0 + ym_ref.shape[1]
    cat_ref[:, 0:c0] = yc_ref[...]
    cat_ref[:, c0:c1] = ym_ref[...]
    cat_ref[:, c1:] = yd_ref[...]
    x1 = x_ref[...] + _dot(cat_ref[...], wout_ref[...])
    x1_ref[...] = x1
    h = _rms(x1, g_ref[...], RMS_EPS)
    if route:
        _store_rows(h_ref, h)
    else:
        h_ref[...] = h.astype(BF16)
    if route:
        n_exp = wr_ref.shape[0]
        tm = h.shape[0]

        @pl.when(pl.program_id(0) == 0)
        def _():
            base_ref[...] = jnp.zeros_like(base_ref)

        lane = lax.broadcasted_iota(jnp.int32, (tm, LANE), 1)
        logits = jnp.full((tm, LANE), NEG_BIG, F32)
        for e in range(n_exp):
            le = jnp.sum(h * wr_ref[e:e + 1, :], axis=-1, keepdims=True)
            logits = jnp.where(lane == e, le, logits)
        m1 = jnp.max(logits, axis=-1, keepdims=True)
        i1 = jnp.min(jnp.where(logits == m1, lane, LANE), axis=-1, keepdims=True)
        rest_l = jnp.where(lane == i1, NEG_BIG, logits)
        m2 = jnp.max(rest_l, axis=-1, keepdims=True)
        i2 = jnp.min(jnp.where(rest_l == m2, lane, LANE), axis=-1, keepdims=True)
        e2 = jnp.exp(m2 - m1)
        w1 = 1.0 / (1.0 + e2)
        w2 = e2 / (1.0 + e2)
        sel = ((lane == i1) | (lane == i2)).astype(F32)
        r = lax.broadcasted_iota(jnp.int32, (tm, tm), 0)
        c = lax.broadcasted_iota(jnp.int32, (tm, tm), 1)
        lower = jnp.where(r > c, 1.0, 0.0).astype(BF16)
        before = _dot(lower, sel.astype(BF16)) + base_ref[...]
        rank1 = jnp.sum(jnp.where(lane == i1, before, 0.0), axis=-1, keepdims=True)
        rank2 = jnp.sum(jnp.where(lane == i2, before, 0.0), axis=-1, keepdims=True)
        total = base_ref[...] + jnp.sum(sel, axis=0, keepdims=True)
        base_ref[...] = total
        cnt_ref[...] = total
        info = jnp.zeros((tm, LANE), F32)
        for ln, val in ((R_IDX, i1.astype(F32)), (R_IDX + 1, i2.astype(F32)), (R_RANK, rank1), (R_RANK + 1, rank2),
                        (R_W, w1), (R_W + 1, w2)):
            info = jnp.where(lane == ln, val, info)
        info_ref[...] = info


def _post_mixer(x2, yc, ym, yd, wout, g, w_router_t, tm):
    n, d = x2.shape
    route = w_router_t is not None
    tok = lambda w: pl.BlockSpec((tm, w), lambda i: (i, 0))
    full = lambda a: pl.BlockSpec(a.shape, lambda i: (0,) * a.ndim)
    dmix = wout.shape[0]
    in_specs = [tok(d), tok(yc.shape[1]), tok(ym.shape[1]), tok(yd.shape[1]), full(wout), full(g)]
    nc = d // LANE
    out_specs = [tok(d), pl.BlockSpec((tm * nc, LANE), lambda i: (i, 0)) if route else tok(d)]
    out_shape = [jax.ShapeDtypeStruct((n, d), F32),
                 jax.ShapeDtypeStruct((n * nc, LANE), F32) if route else jax.ShapeDtypeStruct((n, d), BF16)]
    scratch = [pltpu.VMEM((tm, dmix), BF16)]
    args = [x2, yc, ym, yd, wout, g]
    if route:
        in_specs.append(full(w_router_t))
        out_specs += [tok(LANE), pl.BlockSpec((1, LANE), lambda i: (0, 0))]
        out_shape += [jax.ShapeDtypeStruct((n, LANE), F32), jax.ShapeDtypeStruct((1, LANE), F32)]
        scratch.append(pltpu.VMEM((1, LANE), F32))
        args.append(w_router_t)
    return pl.pallas_call(
        functools.partial(_post_mixer_kernel, route=route),
        grid=(n // tm,),
        in_specs=in_specs,
        out_specs=out_specs,
        out_shape=out_shape,
        scratch_shapes=scratch,
        compiler_params=_params("arbitrary" if route else "parallel"),
        name="post_mixer_route" if route else "post_mixer",
    )(*args)


MXU_DIM = 256


def _col_groups(dff):
    if dff % MXU_DIM == 0 and dff >= 2 * MXU_DIM:
        half = (dff // MXU_DIM + 1) // 2 * MXU_DIM
        return ((0, half), (half, dff))
    return ((0, dff),)


def _swiglu(h, wg_ref, wu_ref, wd_ref):
    out = None
    for lo, hi in _col_groups(wg_ref.shape[-1]):
        a = _dot(h, wg_ref[0, :, lo:hi])
        b = _dot(h, wu_ref[0, :, lo:hi])
        part = _dot((a * jax.nn.sigmoid(a) * b).astype(BF16), wd_ref[0, lo:hi, :])
        out = part if out is None else out + part
    return out


def _ffn_dense_kernel(h_ref, wg_ref, wu_ref, wd_ref, x1_ref, g_ref, o_ref, *, final):
    y = x1_ref[...] + _swiglu(h_ref[...], wg_ref, wu_ref, wd_ref)
    o_ref[...] = _rms(y, g_ref[...], RMS_EPS) if final else y


def _resident(shape, index_map):
    return pl.BlockSpec(shape, index_map, pipeline_mode=pl.Buffered(1))


def _ffn_dense(h, wg, wu, wd, x1, g, tm, final):
    n, d = x1.shape
    dff = wg.shape[-1]
    return pl.pallas_call(
        functools.partial(_ffn_dense_kernel, final=final),
        grid=(n // tm,),
        in_specs=[pl.BlockSpec((tm, d), lambda i: (i, 0)),
                  _resident((1, d, dff), lambda i: (0, 0, 0)),
                  _resident((1, d, dff), lambda i: (0, 0, 0)),
                  _resident((1, dff, d), lambda i: (0, 0, 0)),
                  pl.BlockSpec((tm, d), lambda i: (i, 0)),
                  pl.BlockSpec((1, d), lambda i: (0, 0))],
        out_specs=pl.BlockSpec((tm, d), lambda i: (i, 0)),
        out_shape=jax.ShapeDtypeStruct((n, d), F32),
        compiler_params=_params("parallel"),
        name="ffn_dense",
    )(h, wg, wu, wd, x1, g)


def _row_copies(src_of, dst_of, n_rows, sem):
    def start(r, carry):
        for k in range(TOP_K):
            pltpu.make_async_copy(src_of(r, k), dst_of(r, k), sem).start(priority=k)
        return carry

    def wait(r, carry):
        for k in range(TOP_K):
            pltpu.make_async_copy(src_of(r, k), dst_of(r, k), sem).wait()
        return carry

    lax.fori_loop(0, n_rows, start, 0, unroll=8)
    lax.fori_loop(0, n_rows, wait, 0, unroll=8)


def _row_at(ref, r, nc, n=1, lead=()):
    return ref.at[lead + (pl.ds(pl.multiple_of(r * nc, nc), n * nc),)]


def _scatter_kernel(zstart_ref, pos_ref, h_ref, hs_ref, zeros_ref, sem, *, tmr, nc):
    tm = h_ref.shape[0] // nc

    @pl.when(pl.program_id(0) == 0)
    def _():
        zeros_ref[...] = jnp.zeros_like(zeros_ref)
        for e in range(zstart_ref.shape[0]):
            fill = pltpu.make_async_copy(zeros_ref, _row_at(hs_ref, zstart_ref[e], nc, tmr), sem)
            fill.start()
            fill.wait()

    _row_copies(lambda r, k: _row_at(h_ref, r, nc),
                lambda r, k: _row_at(hs_ref, pos_ref[0, 0, TOP_K * r + k], nc), tm, sem)


def _moe_scatter(zstart, pos3, h, rows, tm, tmr, nc):
    n = h.shape[0] // nc
    grid_spec = pltpu.PrefetchScalarGridSpec(
        num_scalar_prefetch=1,
        grid=(n // tm,),
        in_specs=[pl.BlockSpec((1, 1, TOP_K * tm), lambda i, zs: (i, 0, 0), memory_space=pltpu.SMEM),
                  pl.BlockSpec((tm * nc, LANE), lambda i, zs: (i, 0))],
        out_specs=pl.BlockSpec(memory_space=pl.ANY),
        scratch_shapes=[pltpu.VMEM((tmr * nc, LANE), F32), pltpu.SemaphoreType.DMA(())],
    )
    return pl.pallas_call(
        functools.partial(_scatter_kernel, tmr=tmr, nc=nc),
        grid_spec=grid_spec,
        out_shape=jax.ShapeDtypeStruct((rows * nc, LANE), F32),
        compiler_params=_params("arbitrary"),
        name="moe_scatter",
    )(zstart, pos3, h)


def _ffn_grouped_kernel(te_ref, nv_ref, hs_ref, wg_ref, wu_ref, wd_ref, ys_ref):
    nc = wg_ref.shape[1] // LANE
    tmr = hs_ref.shape[0] // nc

    @pl.when(pl.program_id(0) < nv_ref[0])
    def _():
        h = _load_rows(hs_ref, tmr, nc).astype(BF16)
        _store_rows(ys_ref, _swiglu(h, wg_ref, wu_ref, wd_ref))

    @pl.when(pl.program_id(0) >= nv_ref[0])
    def _():
        ys_ref[...] = jnp.zeros_like(ys_ref)


def _ffn_grouped(tile_expert, n_valid, hs, wg, wu, wd, n_tiles, tmr):
    d = wg.shape[1]
    dff = wg.shape[-1]
    nc = d // LANE
    row_blk = lambda j, te, nv: (jnp.minimum(j, nv[0] - 1), 0)
    grid_spec = pltpu.PrefetchScalarGridSpec(
        num_scalar_prefetch=2,
        grid=(n_tiles,),
        in_specs=[pl.BlockSpec((tmr * nc, LANE), row_blk),
                  _resident((1, d, dff), lambda j, te, nv: (te[j], 0, 0)),
                  _resident((1, d, dff), lambda j, te, nv: (te[j], 0, 0)),
                  _resident((1, dff, d), lambda j, te, nv: (te[j], 0, 0))],
        out_specs=pl.BlockSpec((tmr * nc, LANE), lambda j, te, nv: (j, 0)),
    )
    return pl.pallas_call(
        _ffn_grouped_kernel,
        grid_spec=grid_spec,
        out_shape=jax.ShapeDtypeStruct((n_tiles * tmr * nc, LANE), F32),
        compiler_params=_params("arbitrary"),
        name="ffn_grouped",
    )(tile_expert, n_valid, hs, wg, wu, wd)


def _combine_kernel(pos_ref, x1_ref, info_ref, g_ref, ys_ref, o_ref, ybuf_ref, sem, *, final):
    tm, d = x1_ref.shape
    nc = d // LANE
    _row_copies(lambda r, k: _row_at(ys_ref, pos_ref[0, 0, TOP_K * r + k], nc),
                lambda r, k: _row_at(ybuf_ref, r, nc, lead=(k,)), tm, sem)
    info = info_ref[...]
    y = (x1_ref[...] + info[:, R_W:R_W + 1] * _load_rows(ybuf_ref, tm, nc, (0,))
         + info[:, R_W + 1:R_W + 2] * _load_rows(ybuf_ref, tm, nc, (1,)))
    o_ref[...] = _rms(y, g_ref[...], RMS_EPS) if final else y


def _moe_combine(pos3, x1, info, g, ys, tm, final):
    n, d = x1.shape
    nc = d // LANE
    return pl.pallas_call(
        functools.partial(_combine_kernel, final=final),
        grid=(n // tm,),
        in_specs=[pl.BlockSpec((1, 1, TOP_K * tm), lambda i: (i, 0, 0), memory_space=pltpu.SMEM),
                  pl.BlockSpec((tm, d), lambda i: (i, 0)),
                  pl.BlockSpec((tm, LANE), lambda i: (i, 0)),
                  pl.BlockSpec((1, d), lambda i: (0, 0)),
                  pl.BlockSpec(memory_space=pl.ANY)],
        out_specs=pl.BlockSpec((tm, d), lambda i: (i, 0)),
        out_shape=jax.ShapeDtypeStruct((n, d), F32),
        scratch_shapes=[pltpu.VMEM((TOP_K, tm * nc, LANE), F32), pltpu.SemaphoreType.DMA(())],
        compiler_params=_params("arbitrary"),
        name="moe_combine",
    )(pos3, x1, info, g, ys)


def _moe_plan(info, counts, n_exp, tm, tmr):
    n = info.shape[0]
    idx = info[:, R_IDX:R_IDX + TOP_K].astype(jnp.int32)
    rank = info[:, R_RANK:R_RANK + TOP_K].astype(jnp.int32)
    cnt = counts[0, :n_exp].astype(jnp.int32)
    padded = (cnt + tmr - 1) // tmr * tmr
    ends = jnp.cumsum(padded)
    offs = ends - padded
    pos = rank + jnp.sum(jnp.where(idx[..., None] == jnp.arange(n_exp), offs, 0), axis=-1)
    n_tiles = TOP_K * n // tmr + n_exp
    n_valid = ends[-1] // tmr
    tile_row = jnp.minimum(jnp.arange(n_tiles), n_valid - 1) * tmr
    tile_expert = jnp.sum(tile_row[:, None] >= ends[None, :], axis=-1).astype(jnp.int32)
    pos3 = pos.reshape(n // tm, 1, TOP_K * tm)
    tail = jnp.minimum(ends[-1] + jnp.arange(n_exp + 1) * tmr, n_tiles * tmr)
    zstart = jnp.concatenate([offs + cnt, tail]).astype(jnp.int32)
    return pos3, zstart, tile_expert, n_valid.reshape(1).astype(jnp.int32), n_tiles


def _pick(n, pref):
    t = min(n, pref)
    while n % t:
        t //= 2
    return t


def kernel(x, attn_norm, w_in, b_glu, w_dw, b_dw, conv_ln_g, conv_ln_b, w_pw, b_pw, q_norm, w_uq, kv_norm, w_ukv, lambda_q1, lambda_k1, lambda_q2, lambda_k2, diff_subln, w_out, ffn_norm, w_gate_dense, w_up_dense, w_down_dense, w_router, w_gate_exp, w_up_exp, w_down_exp, final_norm):
    bsz, seq, d = x.shape
    n = bsz * seq
    depth = attn_norm.shape[0]
    dff = w_gate_dense.shape[-1]
    tm = _pick(seq, 512)
    tq = _pick(seq, 256)
    tm_ffn = _pick(n, 512)
    conv_rows = _pick(seq, 256)
    row = lambda a: a.reshape(1, -1)

    tables = _static_tables(seq)
    slope_tab = jnp.broadcast_to(
        (LOG2E * 2.0 ** (-8.0 * jnp.arange(1, DIFF_HEADS + 1, dtype=F32) / DIFF_HEADS))[:, None], (DIFF_HEADS, LANE))

    x2 = x.reshape(n, d)
    for l in range(depth):
        win_p, wuq_p, wukv_p = _prep_in_weights(w_in[l], w_uq[l], w_ukv[l])
        u, q, k, v, dq, dk, dv = _pre_mixer(x2, row(attn_norm[l]), win_p, row(b_glu[l]), row(q_norm[l]), wuq_p,
                                            row(kv_norm[l]), wukv_p, tables, seq, tm)
        r3 = lambda a: a.reshape(bsz, seq, a.shape[-1])
        yc = _conv_module(r3(u), w_dw[l], row(b_dw[l]), row(conv_ln_g[l]), row(conv_ln_b[l]),
                          w_pw[l].astype(BF16), row(b_pw[l]), conv_rows)
        ym = _mla_attention(r3(q), r3(k), r3(v), tq, 2 if bsz % 2 == 0 else 1)
        lam_init = 0.8 - 0.6 * math.exp(-0.3 * l)
        subln2 = jnp.tile(row(diff_subln[l]), (1, LANE // DIFF_V))
        yd = _diff_attention(row(lambda_q1[l]), row(lambda_k1[l]), row(lambda_q2[l]), row(lambda_k2[l]), subln2,
                             slope_tab, r3(dq), r3(dk), r3(dv), tq, lam_init)
        flat = lambda a: a.reshape(n, a.shape[-1])
        final = l == depth - 1
        fin_g = row(final_norm) if final else row(ffn_norm[l])
        j = l // 2
        if l % 2 == 0:
            x1, h2 = _post_mixer(x2, flat(yc), flat(ym), flat(yd), w_out[l].astype(BF16), row(ffn_norm[l]), None, tm)
            x2 = _ffn_dense(h2, w_gate_dense[j][None].astype(BF16), w_up_dense[j][None].astype(BF16),
                            w_down_dense[j][None].astype(BF16), x1, fin_g, tm_ffn, final)
        else:
            x1, h2, info, counts = _post_mixer(x2, flat(yc), flat(ym), flat(yd), w_out[l].astype(BF16),
                                               row(ffn_norm[l]), w_router[j].T, tm)
            n_exp = w_router.shape[-1]
            pos3, zstart, tile_expert, n_valid, n_tiles = _moe_plan(info, counts, n_exp, tm, tm_ffn)
            nc = d // LANE
            assert nc % 8 == 0, "row DMAs need rows that are whole (8, 128) tiles"
            hs = _moe_scatter(zstart, pos3, h2, (n_tiles + 1) * tm_ffn, tm, tm_ffn, nc)
            ys = _ffn_grouped(tile_expert, n_valid, hs, w_gate_exp[j].astype(BF16), w_up_exp[j].astype(BF16),
                              w_down_exp[j].astype(BF16), n_tiles, tm_ffn)
            x2 = _moe_combine(pos3, x1, info, fin_g, ys, tm, final)
    return x2.reshape(bsz, seq, d)
```
